```python
import math
import jax
import jax.numpy as jnp
from jax import lax
import numpy as np

D_MODEL = 2048
BATCH = 16
SEQ = 256
DEPTH = 4
DEC_BATCH = 4
DEC_SEQ = 2048
PAST_LEN = 256

GRID_W = 64
N_EVEN = (DEPTH + 1) // 2
N_ODD = DEPTH // 2
FNET_GROUPS = 4
FNET_GROUP_W = D_MODEL // 8
FNET_WIDTH = FNET_GROUPS * FNET_GROUP_W
GDN_DK = 128
GDN_DV = 128
GDN_HEADS = (D_MODEL // 2) // GDN_DV
GDN_WIDTH = GDN_HEADS * GDN_DV
GDN_QKV = GDN_HEADS * (2 * GDN_DK + GDN_DV)
GDN_CONV = 3
GDN_CHUNK = 64
EVEN_PROJ = FNET_WIDTH + GDN_QKV + GDN_WIDTH + 4 * GDN_HEADS
EVEN_OUT = FNET_WIDTH + GDN_WIDTH
DIFF_DH = 64
DIFF_DV = 2 * DIFF_DH
DIFF_HEADS = D_MODEL // DIFF_DV
DIFF_WIDTH = DIFF_HEADS * DIFF_DV
Q_BLOCK = 128
ROPE_BASE = 10000.0
FFN_DIM = 5632
N_EXPERTS = 8
TOP_K = 2
EXPERT_DIM = 7168
ALPHA = (2 * DEPTH) ** 0.25
DEEPNORM_BETA = (8 * DEPTH) ** -0.25
LN_EPS = 1e-5
NORM_EPS = 1e-6

kernel_name = 'hybrid_fnet_gdn_diffattn_dit_step'


def layer_norm(x, g, b):
    xf = x.astype(jnp.float32)
    mu = jnp.mean(xf, -1, keepdims=True)
    var = jnp.mean(jnp.square(xf - mu), -1, keepdims=True)
    return ((xf - mu) * lax.rsqrt(var + LN_EPS) * g + b).astype(x.dtype)


def rms_norm(x, g):
    xf = x.astype(jnp.float32)
    return xf * lax.rsqrt(jnp.mean(jnp.square(xf), -1, keepdims=True) + NORM_EPS) * g


def l2_normalize(x):
    return x * lax.rsqrt(jnp.sum(jnp.square(x), -1, keepdims=True) + NORM_EPS)


def ada_modulation(cond, w, b):
    m = jax.nn.silu(cond) @ w + b
    return m.reshape(cond.shape[0], 6, 1, D_MODEL)


def modulate(x, shift, scale):
    return x * (1 + scale) + shift


def swiglu(h, w_gu, w_down):
    gu = h @ w_gu
    g, u = jnp.split(gu, 2, axis=-1)
    return (jax.nn.silu(g) * u) @ w_down


def moe_swiglu(h, router_w, w_gu, w_down):
    logits = (h @ router_w).astype(jnp.float32)
    top_val, top_idx = lax.top_k(logits, TOP_K)
    gates = jax.nn.softmax(top_val, axis=-1)
    combine = jnp.sum(jax.nn.one_hot(top_idx, N_EXPERTS, dtype=jnp.float32) * gates[..., None], axis=-2)
    combine = combine.astype(h.dtype)
    out = jnp.zeros_like(h)
    for e in range(N_EXPERTS):
        out = out + combine[..., e:e + 1] * swiglu(h, w_gu[e], w_down[e])
    return out


def fourier_mix(a):
    b, n, _ = a.shape
    a4 = a.astype(jnp.float32).reshape(b, n, FNET_GROUPS, FNET_GROUP_W)
    y = jnp.fft.fftn(a4, axes=(1, 3), norm='ortho').real
    return y.reshape(b, n, FNET_WIDTH).astype(a.dtype)


def centred_depthwise_conv(x, w):
    pad = GDN_CONV // 2
    return lax.conv_general_dilated(x, w[:, None, :], window_strides=(1,), padding=[(pad, pad)],
                                    dimension_numbers=('NWC', 'WIO', 'NWC'),
                                    feature_group_count=x.shape[-1])


def gdn_chunked(q, k, v, g, beta, s0):
    b, n, h, _ = q.shape
    nc = n // GDN_CHUNK

    def to_chunks(t):
        t = t.reshape(b, nc, GDN_CHUNK, *t.shape[2:])
        return jnp.moveaxis(t, 2, 3)

    qc = to_chunks(q * (GDN_DK ** -0.5))
    kc = to_chunks(k)
    vc = to_chunks(v)
    bc = to_chunks(beta)
    gcum = jnp.cumsum(to_chunks(g), axis=-1)
    tril = jnp.tril(jnp.ones((GDN_CHUNK, GDN_CHUNK), dtype=bool))
    strict = jnp.tril(jnp.ones((GDN_CHUNK, GDN_CHUNK), dtype=bool), -1)
    decay = jnp.exp(jnp.where(tril, gcum[..., :, None] - gcum[..., None, :], -jnp.inf))
    kb = kc * bc[..., None]
    lower = jnp.where(strict, jnp.einsum('bnhid,bnhjd->bnhij', kb, kc) * decay, 0.0)
    tri = lower + jnp.eye(GDN_CHUNK, dtype=jnp.float32)
    rhs = jnp.concatenate([vc * bc[..., None], kb * jnp.exp(gcum)[..., None]], axis=-1)
    sol = lax.linalg.triangular_solve(tri, rhs, left_side=True, lower=True, unit_diagonal=True)
    u = sol[..., :GDN_DV]
    w = sol[..., GDN_DV:]
    attn = jnp.einsum('bnhid,bnhjd->bnhij', qc, kc) * decay
    q_dec = qc * jnp.exp(gcum)[..., None]
    g_last = gcum[..., -1]
    k_dec = kc * jnp.exp(g_last[..., None] - gcum)[..., None]

    def step(s, xs):
        u_i, w_i, a_i, qd_i, kd_i, gl_i = xs
        v_new = u_i - jnp.einsum('bhck,bhkv->bhcv', w_i, s)
        o_i = jnp.einsum('bhck,bhkv->bhcv', qd_i, s) + jnp.einsum('bhij,bhjv->bhiv', a_i, v_new)
        s = s * jnp.exp(gl_i)[..., None, None] + jnp.einsum('bhck,bhcv->bhkv', kd_i, v_new)
        return s, o_i

    xs = tuple(jnp.moveaxis(t, 1, 0) for t in (u, w, attn, q_dec, k_dec, g_last))
    s_final, o = lax.scan(step, s0.astype(jnp.float32), xs)
    o = jnp.transpose(o, (1, 0, 3, 2, 4)).reshape(b, n, h, GDN_DV)
    return o, s_final


def flip_time(t):
    return jnp.flip(t, axis=1)


def even_mixer(h, w_in, conv_w, a_log, dt_bias, norm_g, w_out, s0_fwd, s0_bwd):
    b, n, _ = h.shape
    p = h @ w_in
    o1 = FNET_WIDTH
    o2 = o1 + GDN_QKV
    o3 = o2 + GDN_WIDTH
    o4 = o3 + 2 * GDN_HEADS
    a_in, qkv, z, beta_raw, dec_raw = p[..., :o1], p[..., o1:o2], p[..., o2:o3], p[..., o3:o4], p[..., o4:]
    y_a = fourier_mix(a_in)
    qkv = jax.nn.silu(centred_depthwise_conv(qkv, conv_w)).astype(jnp.float32)
    hk = GDN_HEADS * GDN_DK
    q = l2_normalize(qkv[..., :hk].reshape(b, n, GDN_HEADS, GDN_DK))
    k = l2_normalize(qkv[..., hk:2 * hk].reshape(b, n, GDN_HEADS, GDN_DK))
    v = qkv[..., 2 * hk:].reshape(b, n, GDN_HEADS, GDN_DV)
    beta = jax.nn.sigmoid(beta_raw.astype(jnp.float32)).reshape(b, n, 2, GDN_HEADS)
    g = -jnp.exp(a_log.astype(jnp.float32)) * jax.nn.softplus(
        dec_raw.astype(jnp.float32).reshape(b, n, 2, GDN_HEADS) + dt_bias.astype(jnp.float32))
    o_f, s_f = gdn_chunked(q, k, v, g[:, :, 0], beta[:, :, 0], s0_fwd)
    o_b, s_b = gdn_chunked(flip_time(q), flip_time(k), flip_time(v), flip_time(g[:, :, 1]),
                           flip_time(beta[:, :, 1]), s0_bwd)
    o = o_f + flip_time(o_b)
    zg = jax.nn.silu(z.astype(jnp.float32)).reshape(b, n, GDN_HEADS, GDN_DV)
    y_b = (rms_norm(o, norm_g) * zg).reshape(b, n, GDN_WIDTH).astype(h.dtype)
    y = jnp.concatenate([y_a, y_b], axis=-1) @ w_out
    return y, s_f, s_b


def diff_qkv(h, w_in):
    b, n, _ = h.shape
    p = h @ w_in
    q = p[..., :DIFF_WIDTH].reshape(b, n, 2 * DIFF_HEADS, DIFF_DH)
    k = p[..., DIFF_WIDTH:2 * DIFF_WIDTH].reshape(b, n, 2 * DIFF_HEADS, DIFF_DH)
    v = p[..., 2 * DIFF_WIDTH:].reshape(b, n, DIFF_HEADS, DIFF_DV)
    return q, k, v


def diff_lambda(lq1, lk1, lq2, lk2, lam_init):
    f32 = jnp.float32
    return (jnp.exp(jnp.sum(lq1.astype(f32) * lk1.astype(f32)))
            - jnp.exp(jnp.sum(lq2.astype(f32) * lk2.astype(f32))) + lam_init)


def diff_attention(q, k, v, lam):
    b, n = q.shape[:2]
    nb = n // Q_BLOCK
    qb = jnp.swapaxes(q.reshape(b, nb, Q_BLOCK, 2 * DIFF_HEADS, DIFF_DH), 0, 1)
    vf = v.astype(jnp.float32)
    scale = DIFF_DH ** -0.5

    def one_block(qi):
        s = jnp.einsum('bqhd,bkhd->bhqk', qi, k).astype(jnp.float32) * scale
        pr = jax.nn.softmax(s, axis=-1).reshape(b, DIFF_HEADS, 2, Q_BLOCK, -1)
        a = pr[:, :, 0] - lam * pr[:, :, 1]
        return jnp.einsum('bhqk,bkhe->bqhe', a, vf)

    o = lax.map(one_block, qb)
    return jnp.swapaxes(o, 0, 1).reshape(b, n, DIFF_HEADS, DIFF_DV)


def diff_out(o, norm_g, lam_init, w_out, dtype):
    b, n = o.shape[:2]
    o = rms_norm(o, norm_g) * (1.0 - lam_init)
    return o.reshape(b, n, DIFF_WIDTH).astype(dtype) @ w_out


def axial_rope_tables(n_tokens):
    rows = n_tokens // GRID_W
    t = jnp.arange(rows * GRID_W)
    row = (t // GRID_W).astype(jnp.float32)
    col = (t % GRID_W).astype(jnp.float32)
    axis_dim = DIFF_DH // 2
    inv_freq = 1.0 / (ROPE_BASE ** (jnp.arange(0, axis_dim, 2, dtype=jnp.float32) / axis_dim))
    ang = jnp.stack([row[:, None] * inv_freq, col[:, None] * inv_freq], axis=1)
    return jnp.cos(ang), jnp.sin(ang)


def apply_axial_rope(x, cos, sin):
    axis_dim = DIFF_DH // 2
    half = axis_dim // 2
    xa = x.astype(jnp.float32).reshape(*x.shape[:-1], 2, axis_dim)
    x1, x2 = xa[..., :half], xa[..., half:]
    c = cos[:, None]
    s = sin[:, None]
    out = jnp.concatenate([x1 * c - x2 * s, x2 * c + x1 * s], axis=-1)
    return out.reshape(x.shape).astype(x.dtype)


def setup_inputs(seed: int = 0) -> dict:
    key = jax.random.key(seed)
    ks = jax.random.split(key, 32)
    f32 = jnp.float32
    D = D_MODEL

    def nrm(k, shape, s):
        return jax.random.normal(k, shape, f32) * s

    dt = jnp.exp(jax.random.uniform(ks[14], (N_EVEN, 2, GDN_HEADS), f32, math.log(1e-3), math.log(1e-1)))
    return {
        'x_prompt': nrm(ks[0], (BATCH, SEQ, D), 1.0),
        'x_sample': nrm(ks[1], (DEC_BATCH, DEC_SEQ, D), 1.0),
        'state_gdn': nrm(ks[2], (DEC_BATCH, N_EVEN, 2, GDN_HEADS, GDN_DK, GDN_DV), 0.1),
        'cache_diff_k': nrm(ks[3], (DEC_BATCH, N_ODD, PAST_LEN, 2 * DIFF_HEADS, DIFF_DH), 1.0),
        'cache_diff_v': nrm(ks[4], (DEC_BATCH, N_ODD, PAST_LEN, DIFF_HEADS, DIFF_DV), 1.0),
        'c': nrm(ks[5], (DEC_BATCH, D), 1.0),
        'c_ctx': nrm(ks[6], (D,), 1.0),
        'w_ada': nrm(ks[7], (DEPTH, D, 6 * D), 0.5 * D ** -0.5),
        'b_ada': nrm(ks[8], (DEPTH, 6 * D), 0.02),
        'ln_g': 1.0 + nrm(ks[9], (DEPTH, 2, D), 0.02),
        'ln_b': nrm(ks[10], (DEPTH, 2, D), 0.02),
        'w_in_even': nrm(ks[11], (N_EVEN, D, EVEN_PROJ), D ** -0.5),
        'gdn_conv_w': nrm(ks[12], (N_EVEN, GDN_CONV, GDN_QKV), GDN_CONV ** -0.5),
        'gdn_a_log': jnp.log(jax.random.uniform(ks[13], (N_EVEN, 2, GDN_HEADS), f32, 1.0, 16.0)),
        'gdn_dt_bias': dt + jnp.log(-jnp.expm1(-dt)),
        'gdn_norm_g': 1.0 + nrm(ks[15], (N_EVEN, GDN_DV), 0.02),
        'w_out_even': nrm(ks[16], (N_EVEN, EVEN_OUT, D), DEEPNORM_BETA * EVEN_OUT ** -0.5),
        'w_in_odd': nrm(ks[17], (N_ODD, D, 3 * DIFF_WIDTH), D ** -0.5),
        'lam_q1': nrm(ks[18], (N_ODD, DIFF_DH), 0.1),
        'lam_k1': nrm(ks[19], (N_ODD, DIFF_DH), 0.1),
        'lam_q2': nrm(ks[20], (N_ODD, DIFF_DH), 0.1),
        'lam_k2': nrm(ks[21], (N_ODD, DIFF_DH), 0.1),
        'diff_norm_g': 1.0 + nrm(ks[22], (N_ODD, DIFF_DV), 0.02),
        'w_out_odd': nrm(ks[23], (N_ODD, DIFF_WIDTH, D), DEEPNORM_BETA * DIFF_WIDTH ** -0.5),
        'ffn_w_gu': nrm(ks[24], (N_EVEN, D, 2 * FFN_DIM), D ** -0.5),
        'ffn_w_down': nrm(ks[25], (N_EVEN, FFN_DIM, D), DEEPNORM_BETA * FFN_DIM ** -0.5),
        'router_w': nrm(ks[26], (N_ODD, D, N_EXPERTS), D ** -0.5),
        'moe_w_gu': nrm(ks[27], (N_ODD, N_EXPERTS, D, 2 * EXPERT_DIM), D ** -0.5),
        'moe_w_down': nrm(ks[28], (N_ODD, N_EXPERTS, EXPERT_DIM, D), DEEPNORM_BETA * EXPERT_DIM ** -0.5),
    }


def reference(x_prompt, x_sample, state_gdn, cache_diff_k, cache_diff_v, c, c_ctx, w_ada, b_ada, ln_g, ln_b,
              w_in_even, gdn_conv_w, gdn_a_log, gdn_dt_bias, gdn_norm_g, w_out_even, w_in_odd,
              lam_q1, lam_k1, lam_q2, lam_k2, diff_norm_g, w_out_odd, ffn_w_gu, ffn_w_down,
              router_w, moe_w_gu, moe_w_down):
    n_lat = x_sample.shape[1]
    rope_cos, rope_sin = axial_rope_tables(n_lat)
    xp, xs = x_prompt, x_sample
    gdn_states, ctx_keys, ctx_vals = [], [], []
    for l in range(DEPTH):
        i = l // 2
        m_ctx = ada_modulation(c_ctx[None, :], w_ada[l], b_ada[l])
        m_lat = ada_modulation(c, w_ada[l], b_ada[l])
        hp = modulate(xp, m_ctx[:, 0], m_ctx[:, 1])
        hs = modulate(xs, m_lat[:, 0], m_lat[:, 1])
        if l % 2 == 0:
            zero = jnp.zeros((xp.shape[0], GDN_HEADS, GDN_DK, GDN_DV), jnp.float32)
            yp, s_f, s_b = even_mixer(hp, w_in_even[i], gdn_conv_w[i], gdn_a_log[i], gdn_dt_bias[i],
                                      gdn_norm_g[i], w_out_even[i], zero, zero)
            ys, _, _ = even_mixer(hs, w_in_even[i], gdn_conv_w[i], gdn_a_log[i], gdn_dt_bias[i],
                                  gdn_norm_g[i], w_out_even[i], state_gdn[:, i, 0], state_gdn[:, i, 1])
            gdn_states.append(jnp.stack([s_f, s_b], axis=1).astype(x_prompt.dtype))
        else:
            lam_init = 0.8 - 0.6 * math.exp(-0.3 * l)
            lam = diff_lambda(lam_q1[i], lam_k1[i], lam_q2[i], lam_k2[i], lam_init)
            qp, kp, vp = diff_qkv(hp, w_in_odd[i])
            yp = diff_out(diff_attention(qp, kp, vp, lam), diff_norm_g[i], lam_init, w_out_odd[i], xp.dtype)
            ctx_keys.append(kp)
            ctx_vals.append(vp)
            qs, kls, vls = diff_qkv(hs, w_in_odd[i])
            qs = apply_axial_rope(qs, rope_cos, rope_sin)
            kls = apply_axial_rope(kls, rope_cos, rope_sin)
            k_all = jnp.concatenate([cache_diff_k[:, i].astype(kls.dtype), kls], axis=1)
            v_all = jnp.concatenate([cache_diff_v[:, i].astype(vls.dtype), vls], axis=1)
            ys = diff_out(diff_attention(qs, k_all, v_all, lam), diff_norm_g[i], lam_init, w_out_odd[i], xs.dtype)
        xp = layer_norm(ALPHA * xp + m_ctx[:, 2] * yp, ln_g[l, 0], ln_b[l, 0])
        xs = layer_norm(ALPHA * xs + m_lat[:, 2] * ys, ln_g[l, 0], ln_b[l, 0])
        hp = modulate(xp, m_ctx[:, 3], m_ctx[:, 4])
        hs = modulate(xs, m_lat[:, 3], m_lat[:, 4])
        if l % 2 == 0:
            fp = swiglu(hp, ffn_w_gu[i], ffn_w_down[i])
            fs = swiglu(hs, ffn_w_gu[i], ffn_w_down[i])
        else:
            fp = moe_swiglu(hp, router_w[i], moe_w_gu[i], moe_w_down[i])
            fs = moe_swiglu(hs, router_w[i], moe_w_gu[i], moe_w_down[i])
        xp = layer_norm(ALPHA * xp + m_ctx[:, 5] * fp, ln_g[l, 1], ln_b[l, 1])
        xs = layer_norm(ALPHA * xs + m_lat[:, 5] * fs, ln_g[l, 1], ln_b[l, 1])
    new_state_gdn = jnp.stack(gdn_states, axis=1)
    new_cache_diff_k = jnp.stack(ctx_keys, axis=1)
    new_cache_diff_v = jnp.stack(ctx_vals, axis=1)
    return (xp, xs, new_state_gdn, new_cache_diff_k, new_cache_diff_v)
```

```python
import functools
import math

import numpy as np
import jax
import jax.numpy as jnp
from jax import lax
from jax.experimental import pallas as pl
from jax.experimental.pallas import tpu as pltpu

F32 = jnp.float32
BF16 = jnp.bfloat16
HI = lax.Precision.HIGHEST

D_MODEL = 2048
BATCH = 16
SEQ = 256
DEPTH = 4
DEC_BATCH = 4
DEC_SEQ = 2048
PAST_LEN = 256
GRID_W = 64
N_EVEN = (DEPTH + 1) // 2
N_ODD = DEPTH // 2
FNET_GROUPS = 4
FNET_GROUP_W = D_MODEL // 8
FNET_WIDTH = FNET_GROUPS * FNET_GROUP_W
GDN_DK = 128
GDN_DV = 128
GDN_HEADS = (D_MODEL // 2) // GDN_DV
GDN_WIDTH = GDN_HEADS * GDN_DV
GDN_QKV = GDN_HEADS * (2 * GDN_DK + GDN_DV)
GDN_CONV = 3
EVEN_MAIN = FNET_WIDTH + GDN_QKV + GDN_WIDTH
DIFF_DH = 64
DIFF_DV = 2 * DIFF_DH
DIFF_HEADS = D_MODEL // DIFF_DV
DIFF_WIDTH = DIFF_HEADS * DIFF_DV
ROPE_BASE = 10000.0
FFN_DIM = 5632
N_EXPERTS = 8
TOP_K = 2
EXPERT_DIM = 7168
ALPHA = (2 * DEPTH) ** 0.25
LN_EPS = 1e-5
NORM_EPS = 1e-6

N_P = BATCH * SEQ
N_S = DEC_BATCH * DEC_SEQ
N_TOK = N_P + N_S
N_GROUPS = 1 + DEC_BATCH

LANES = 128
VMEM_LIMIT = 56 * 1024 * 1024
GDN_CHUNK = 64
GDN_TB = 256
TM_PROJ = 1024
TN_PROJ = 1024
TM_LN = 512
TM_FFN = 512
TF_FFN = 512
TM_MOE = 512
TM_TOK = 256
TQ_ATT = 256
TI_FNET = 512
MOE_ROWS = TOP_K * N_TOK + N_EXPERTS * TM_MOE
MOE_TILES = MOE_ROWS // TM_MOE


def _cparams(sem):
    return pltpu.CompilerParams(dimension_semantics=sem, vmem_limit_bytes=VMEM_LIMIT)


def _group_of_row(r0):
    return jnp.where(r0 < N_P, 0, 1 + (r0 - N_P) // DEC_SEQ)


def _sigmoid(x):
    return 1.0 / (1.0 + jnp.exp(-x))


def _softplus(x):
    return jnp.maximum(x, 0.0) + jnp.log1p(jnp.exp(-jnp.abs(x)))


def _dot(a, b, precision=None):
    return jnp.dot(a, b, preferred_element_type=F32, precision=precision)


def _dot_nt(a, b):
    return lax.dot_general(a, b, (((1,), (1,)), ((), ())), preferred_element_type=F32)


def _dot_tn(a, b):
    return lax.dot_general(a, b, (((0,), (0,)), ((), ())), preferred_element_type=F32)


def _layer_norm(r, g, b):
    mu = jnp.mean(r, axis=-1, keepdims=True)
    d = r - mu
    var = jnp.mean(d * d, axis=-1, keepdims=True)
    return d * lax.rsqrt(var + LN_EPS) * g + b


def _ada_kernel(cond_ref, w_ref, b_ref, o_ref):
    c = cond_ref[...]
    s = c * _sigmoid(c)
    o_ref[...] = _dot(s, w_ref[...], HI) + b_ref[...]


def _ada_modulation(cond8, w_ada, b_ada):
    tn = 1024
    return pl.pallas_call(
        _ada_kernel,
        grid=(DEPTH, 6 * D_MODEL // tn),
        in_specs=[
            pl.BlockSpec((8, D_MODEL), lambda l, j: (0, 0)),
            pl.BlockSpec((None, D_MODEL, tn), lambda l, j: (l, 0, j)),
            pl.BlockSpec((None, 1, tn), lambda l, j: (l, 0, j)),
        ],
        out_specs=pl.BlockSpec((None, 8, tn), lambda l, j: (l, 0, j)),
        out_shape=jax.ShapeDtypeStruct((DEPTH, 8, 6 * D_MODEL), F32),
        compiler_params=_cparams(("arbitrary", "arbitrary")),
        name="ada_modulation",
    )(cond8, w_ada, b_ada.reshape(DEPTH, 1, 6 * D_MODEL))


def _modulated(x_ref, mod_ref, shift_row):
    m = mod_ref[...]
    return x_ref[...] * (1.0 + m[shift_row + 1:shift_row + 2, :]) + m[shift_row:shift_row + 1, :]


def _inproj_even_kernel(x_ref, mod_ref, w_ref, wt_ref, wtt_ref, o_ref, ot_ref, ott_ref, h_scr):
    @pl.when(pl.program_id(1) == 0)
    def _():
        hb = _modulated(x_ref, mod_ref, 0).astype(BF16)
        h_scr[...] = hb
        ot_ref[...] = _dot(hb, wt_ref[...])
        ott_ref[...] = _dot_nt(wtt_ref[...], hb)

    o_ref[...] = _dot(h_scr[...], w_ref[...])


def _inproj_even(x, mod_all, layer, w_main, w_tail, w_tail_t, il):
    tm, tn = TM_PROJ, TN_PROJ
    return pl.pallas_call(
        _inproj_even_kernel,
        grid=(N_TOK // tm, EVEN_MAIN // tn),
        in_specs=[
            pl.BlockSpec((tm, D_MODEL), lambda i, j: (i, 0)),
            pl.BlockSpec((None, None, 8, D_MODEL), lambda i, j: (layer, _group_of_row(i * tm), 0, 0)),
            pl.BlockSpec((None, D_MODEL, tn), lambda i, j: (il, 0, j)),
            pl.BlockSpec((None, D_MODEL, LANES), lambda i, j: (il, 0, 0)),
            pl.BlockSpec((None, LANES, D_MODEL), lambda i, j: (il, 0, 0)),
        ],
        out_specs=[
            pl.BlockSpec((tm, tn), lambda i, j: (i, j)),
            pl.BlockSpec((tm, LANES), lambda i, j: (i, 0)),
            pl.BlockSpec((LANES, tm), lambda i, j: (0, i)),
        ],
        out_shape=[
            jax.ShapeDtypeStruct((N_TOK, EVEN_MAIN), F32),
            jax.ShapeDtypeStruct((N_TOK, LANES), F32),
            jax.ShapeDtypeStruct((LANES, N_TOK), F32),
        ],
        scratch_shapes=[pltpu.VMEM((tm, D_MODEL), BF16)],
        compiler_params=_cparams(("arbitrary", "arbitrary")),
        name="inproj_even",
    )(x, mod_all, w_main, w_tail, w_tail_t)


def _inproj_odd_kernel(x_ref, mod_ref, w_ref, cos_ref, sa_ref, sb_ref, o_ref, h_scr, *, tm, tn):
    i = pl.program_id(0)
    j = pl.program_id(1)

    @pl.when(j == 0)
    def _():
        h_scr[...] = _modulated(x_ref, mod_ref, 0).astype(BF16)

    y = _dot(h_scr[...], w_ref[...])
    is_rope = jnp.logical_and(i * tm >= N_P, j * tn < 2 * DIFF_WIDTH)

    @pl.when(is_rope)
    def _():
        c = cos_ref[...]
        sa = sa_ref[...]
        sb = sb_ref[...]
        for cb in range(tn // LANES):
            yc = y[:, cb * LANES:(cb + 1) * LANES]
            o_ref[:, cb * LANES:(cb + 1) * LANES] = (
                yc * c + pltpu.roll(yc, 16, 1) * sa + pltpu.roll(yc, LANES - 16, 1) * sb)

    @pl.when(jnp.logical_not(is_rope))
    def _():
        o_ref[...] = y


def _inproj_odd(x, mod_all, layer, w_in, il, rope):
    tm, tn = TM_PROJ, TN_PROJ

    def rope_map(i, j):
        r0 = i * tm
        return (jnp.where(r0 >= N_P, ((r0 - N_P) % DEC_SEQ) // tm, 0), 0)

    return pl.pallas_call(
        functools.partial(_inproj_odd_kernel, tm=tm, tn=tn),
        grid=(N_TOK // tm, 3 * DIFF_WIDTH // tn),
        in_specs=[
            pl.BlockSpec((tm, D_MODEL), lambda i, j: (i, 0)),
            pl.BlockSpec((None, None, 8, D_MODEL), lambda i, j: (layer, _group_of_row(i * tm), 0, 0)),
            pl.BlockSpec((None, D_MODEL, tn), lambda i, j: (il, 0, j)),
            pl.BlockSpec((tm, LANES), rope_map),
            pl.BlockSpec((tm, LANES), rope_map),
            pl.BlockSpec((tm, LANES), rope_map),
        ],
        out_specs=pl.BlockSpec((tm, tn), lambda i, j: (i, j)),
        out_shape=jax.ShapeDtypeStruct((N_TOK, 3 * DIFF_WIDTH), F32),
        scratch_shapes=[pltpu.VMEM((tm, D_MODEL), BF16)],
        compiler_params=_cparams(("arbitrary", "arbitrary")),
        name="inproj_odd",
    )(x, mod_all, w_in, *rope)


def _rope_tables():
    t = np.arange(DEC_SEQ)
    pos = np.stack([t // GRID_W, t % GRID_W], axis=1).astype(np.float32)
    axis_dim = DIFF_DH // 2
    half = axis_dim // 2
    inv_freq = (1.0 / (ROPE_BASE ** (np.arange(0, axis_dim, 2, dtype=np.float32) / axis_dim))).astype(np.float32)
    lane = np.arange(LANES)
    axis = (lane % DIFF_DH) // axis_dim
    within = lane % axis_dim
    ang = (pos[:, axis] * inv_freq[within % half][None, :]).astype(np.float32)
    cos = np.cos(ang).astype(np.float32)
    sin = np.sin(ang).astype(np.float32)
    second = (within >= half)[None, :]
    sa = np.where(second, sin, 0.0).astype(np.float32)
    sb = np.where(second, 0.0, -sin).astype(np.float32)
    return jnp.asarray(cos), jnp.asarray(sa), jnp.asarray(sb)


def _dft_tables(n):
    def cs(m):
        jk = (np.arange(m)[:, None] * np.arange(m)[None, :]) % m
        ang = 2.0 * np.pi * jk.astype(np.float64) / m
        return np.cos(ang), np.sin(ang)

    cc, sc = cs(FNET_GROUP_W)
    cn, sn = cs(n)
    chan = np.concatenate([cc, sc], axis=1)
    posn = np.concatenate([cn, -sn], axis=1)
    return jnp.asarray(chan, dtype=BF16), jnp.asarray(posn, dtype=BF16)


def _fnet_kernel(a_ref, chan_ref, posn_ref, o_ref, t_scr, *, n):
    @pl.when(pl.program_id(1) == 0)
    def _():
        a = a_ref[...].astype(BF16)
        w = FNET_GROUP_W
        for g in range(FNET_GROUPS):
            t = _dot(a[:, g * w:(g + 1) * w], chan_ref[...])
            t_scr[0:n, g * w:(g + 1) * w] = t[:, :w].astype(BF16)
            t_scr[n:2 * n, g * w:(g + 1) * w] = t[:, w:].astype(BF16)

    y = _dot(posn_ref[...], t_scr[...])
    o_ref[...] = (y * ((n * FNET_GROUP_W) ** -0.5)).astype(BF16)


def _fnet(p_main, row0, nb, n, y_prev=None):
    ti = min(TI_FNET, n)
    nt = n // ti
    chan, posn = _dft_tables(n)
    out_shape = jax.ShapeDtypeStruct((N_TOK, FNET_WIDTH), BF16)
    args = [p_main, chan, posn]
    in_specs = [
        pl.BlockSpec((n, FNET_WIDTH), lambda b, i: (row0 // n + b, 0)),
        pl.BlockSpec((FNET_GROUP_W, 2 * FNET_GROUP_W), lambda b, i: (0, 0)),
        pl.BlockSpec((ti, 2 * n), lambda b, i: (i, 0)),
    ]
    kwargs = {}
    kern = functools.partial(_fnet_kernel, n=n)
    if y_prev is not None:
        args.append(y_prev)
        in_specs.append(pl.BlockSpec(memory_space=pl.ANY))
        kwargs["input_output_aliases"] = {3: 0}
        kern = functools.partial(_fnet_alias_kernel, n=n)
    return pl.pallas_call(
        kern,
        grid=(nb, nt),
        in_specs=in_specs,
        out_specs=pl.BlockSpec((ti, FNET_WIDTH), lambda b, i: (row0 // ti + b * nt + i, 0)),
        out_shape=out_shape,
        scratch_shapes=[pltpu.VMEM((2 * n, FNET_WIDTH), BF16)],
        compiler_params=_cparams(("arbitrary", "arbitrary")),
        name=f"fnet_{n}",
        **kwargs,
    )(*args)


def _fnet_alias_kernel(a_ref, chan_ref, posn_ref, prev_ref, o_ref, t_scr, *, n):
    del prev_ref
    _fnet_kernel(a_ref, chan_ref, posn_ref, o_ref, t_scr, n=n)


def _gdn_prep_kernel(x_ref, w_ref, o_ref):
    j = pl.program_id(1)
    x = x_ref[...]
    n = x.shape[0]
    row = lax.broadcasted_iota(jnp.int32, x.shape, 0)
    x_prev = jnp.where(row == 0, 0.0, pltpu.roll(x, 1, 0))
    x_next = jnp.where(row == n - 1, 0.0, pltpu.roll(x, n - 1, 0))
    w = w_ref[...]
    y = x_prev * w[0:1, :] + x * w[1:2, :] + x_next * w[2:3, :]
    y = y * _sigmoid(y)
    inv = lax.rsqrt(jnp.sum(y * y, axis=-1, keepdims=True) + NORM_EPS)
    scale = jnp.where(j < GDN_HEADS, inv * (GDN_DK ** -0.5), jnp.where(j < 2 * GDN_HEADS, inv, 1.0))
    o_ref[...] = y * scale


def _gdn_prep(p_main, conv_w, il, row0, nb, n, prev=None):
    col0 = FNET_WIDTH // LANES
    args = [p_main, conv_w]
    in_specs = [
        pl.BlockSpec((n, LANES), lambda b, j: (row0 // n + b, col0 + j)),
        pl.BlockSpec((None, GDN_CONV, LANES), lambda b, j: (il, 0, j)),
    ]
    kwargs = {}
    kern = _gdn_prep_kernel
    if prev is not None:
        args.append(prev)
        in_specs.append(pl.BlockSpec(memory_space=pl.ANY))
        kwargs["input_output_aliases"] = {2: 0}
        kern = _gdn_prep_alias_kernel
    return pl.pallas_call(
        kern,
        grid=(nb, GDN_QKV // LANES),
        in_specs=in_specs,
        out_specs=pl.BlockSpec((n, LANES), lambda b, j: (row0 // n + b, j)),
        out_shape=jax.ShapeDtypeStruct((N_TOK, GDN_QKV), F32),
        compiler_params=_cparams(("arbitrary", "arbitrary")),
        name=f"gdn_prep_{n}",
        **kwargs,
    )(*args)


def _gdn_prep_alias_kernel(x_ref, w_ref, prev_ref, o_ref):
    del prev_ref
    _gdn_prep_kernel(x_ref, w_ref, o_ref)


def _unit_tri_inverse(low, eye):
    p = -low
    x = eye + p
    c = low.shape[0]
    for _ in range(int(math.log2(c)) - 1):
        p = _dot(p, p, HI)
        x = x + _dot(x, p, HI)
    return x


def _gdn_kernel(*refs, rev, has_s0, finish, emit_state, ntb):
    it = iter(refs)
    q_ref, k_ref, v_ref, tail_ref, tailt_ref = next(it), next(it), next(it), next(it), next(it)
    arow_ref, dtrow_ref, acol_ref, dtcol_ref = next(it), next(it), next(it), next(it)
    s0_ref = next(it) if has_s0 else None
    if finish:
        of_ref, z_ref, ng_ref = next(it), next(it), next(it)
    o_ref = next(it)
    sfin_ref = next(it) if emit_state else None
    s_scr = next(it)

    h = pl.program_id(1)
    tb = pl.program_id(2)
    d = 1 if rev else 0
    c = GDN_CHUNK
    ncb = GDN_TB // c

    @pl.when(tb == 0)
    def _():
        if has_s0:
            s_scr[...] = s0_ref[...]
        else:
            s_scr[...] = jnp.zeros_like(s_scr)

    gcol = 2 * GDN_HEADS + d * GDN_HEADS + h
    bcol = d * GDN_HEADS + h
    tail = tail_ref[...]
    lane = lax.broadcasted_iota(jnp.int32, tail.shape, 1)
    g_all = -jnp.exp(arow_ref[...]) * _softplus(tail + dtrow_ref[...])
    g_colv = jnp.sum(jnp.where(lane == gcol, g_all, 0.0), axis=1, keepdims=True)
    beta_colv = jnp.sum(jnp.where(lane == bcol, _sigmoid(tail), 0.0), axis=1, keepdims=True)
    tail_t = tailt_ref[...]
    sub = lax.broadcasted_iota(jnp.int32, tail_t.shape, 0)
    g_all_t = -jnp.exp(acol_ref[...]) * _softplus(tail_t + dtcol_ref[...])
    g_rowv = jnp.sum(jnp.where(sub == gcol, g_all_t, 0.0), axis=0, keepdims=True)

    ri = lax.broadcasted_iota(jnp.int32, (c, c), 0)
    ci = lax.broadcasted_iota(jnp.int32, (c, c), 1)
    eye = (ri == ci).astype(F32)
    if rev:
        m_incl = ri <= ci
        m_strict = ri < ci
    else:
        m_incl = ri >= ci
        m_strict = ri > ci
    cum_col = m_incl.astype(F32)
    cum_row = ((ri >= ci) if rev else (ri <= ci)).astype(F32)

    chunk_order = list(range(ncb - 1, -1, -1)) if rev else list(range(ncb))

    local = {}
    for cc in chunk_order:
        rows = slice(cc * c, (cc + 1) * c)
        q = q_ref[rows, :]
        k = k_ref[rows, :]
        v = v_ref[rows, :]
        g_c = jnp.broadcast_to(g_colv[rows, :], (c, LANES))
        gcum = _dot(cum_col, g_c, HI)
        g_r = jnp.broadcast_to(g_rowv[:, cc * c:(cc + 1) * c], (8, c))
        gcum_r = _dot(g_r, cum_row, HI)[0:1, :]
        g_last = gcum[0:1, :] if rev else gcum[c - 1:c, :]
        diff = gcum[:, :c] - gcum_r
        decay = jnp.where(m_incl, jnp.exp(jnp.minimum(diff, 0.0)), 0.0)
        beta = jnp.broadcast_to(beta_colv[rows, :], (c, LANES))
        kb = k * beta
        kbf = k.astype(BF16)
        kk = _dot_nt(kb.astype(BF16), kbf)
        qk = _dot_nt(q.astype(BF16), kbf)
        low = jnp.where(m_strict, kk * decay, 0.0)
        attn = qk * decay
        tinv = _unit_tri_inverse(low, eye)
        e_g = jnp.exp(gcum)
        u = _dot(tinv, v * beta, HI)
        w = _dot(tinv, kb * e_g, HI)
        q_dec = q * e_g
        k_dec = k * jnp.exp(g_last - gcum)
        local[cc] = (u, w, attn, q_dec, k_dec, jnp.exp(g_last))

    s = s_scr[...]
    for cc in chunk_order:
        rows = slice(cc * c, (cc + 1) * c)
        u, w, attn, q_dec, k_dec, e_last = local[cc]
        sb = s.astype(BF16)
        v_new = u - _dot(w.astype(BF16), sb)
        vb = v_new.astype(BF16)
        o = _dot(q_dec.astype(BF16), sb) + _dot(attn.astype(BF16), vb)
        s = s * e_last + _dot_tn(k_dec.astype(BF16), vb)
        if finish:
            o = o + of_ref[rows, :]
            z = z_ref[rows, :]
            o = o * lax.rsqrt(jnp.mean(o * o, axis=-1, keepdims=True) + NORM_EPS) * ng_ref[...]
            o_ref[rows, :] = (o * (z * _sigmoid(z))).astype(o_ref.dtype)
        else:
            o_ref[rows, :] = o
    s_scr[...] = s

    if emit_state:
        @pl.when(tb == ntb - 1)
        def _():
            sfin_ref[...] = s


def _gdn_scan(qkv, tail, tail_t, gate_params, il, row0, nb, n, *, rev, s0=None, o_fwd=None, z_src=None,
              norm_g=None, emit_state=False, prev=None):
    tb_sz = GDN_TB
    ntb = n // tb_sz
    finish = o_fwd is not None
    d = 1 if rev else 0
    a_row, dt_row, a_col, dt_col = gate_params

    def rowblk(b, t):
        tt = (ntb - 1 - t) if rev else t
        return row0 // tb_sz + b * ntb + tt

    args = [qkv, qkv, qkv, tail, tail_t, a_row, dt_row, a_col, dt_col]
    in_specs = [
        pl.BlockSpec((tb_sz, LANES), lambda b, h, t: (rowblk(b, t), h)),
        pl.BlockSpec((tb_sz, LANES), lambda b, h, t: (rowblk(b, t), GDN_HEADS + h)),
        pl.BlockSpec((tb_sz, LANES), lambda b, h, t: (rowblk(b, t), 2 * GDN_HEADS + h)),
        pl.BlockSpec((tb_sz, LANES), lambda b, h, t: (rowblk(b, t), 0)),
        pl.BlockSpec((LANES, tb_sz), lambda b, h, t: (0, rowblk(b, t))),
        pl.BlockSpec((None, 1, LANES), lambda b, h, t: (il, 0, 0)),
        pl.BlockSpec((None, 1, LANES), lambda b, h, t: (il, 0, 0)),
        pl.BlockSpec((None, LANES, 1), lambda b, h, t: (il, 0, 0)),
        pl.BlockSpec((None, LANES, 1), lambda b, h, t: (il, 0, 0)),
    ]
    if s0 is not None:
        args.append(s0)
        in_specs.append(pl.BlockSpec((None, None, None, None, GDN_DK, GDN_DV),
                                     lambda b, h, t: (b, il, d, h, 0, 0)))
    if finish:
        zcol0 = (FNET_WIDTH + GDN_QKV) // LANES
        args += [o_fwd, z_src, norm_g]
        in_specs += [
            pl.BlockSpec((tb_sz, LANES), lambda b, h, t: (rowblk(b, t), h)),
            pl.BlockSpec((tb_sz, LANES), lambda b, h, t: (rowblk(b, t), zcol0 + h)),
            pl.BlockSpec((None, 1, GDN_DV), lambda b, h, t: (il, 0, 0)),
        ]
    out_dtype = BF16 if finish else F32
    out_shape = [jax.ShapeDtypeStruct((N_TOK, GDN_WIDTH), out_dtype)]
    out_specs = [pl.BlockSpec((tb_sz, LANES), lambda b, h, t: (rowblk(b, t), h))]
    if emit_state:
        out_shape.append(jax.ShapeDtypeStruct((nb, GDN_HEADS, GDN_DK, GDN_DV), F32))
        out_specs.append(pl.BlockSpec((None, None, GDN_DK, GDN_DV), lambda b, h, t: (b, h, 0, 0)))
    kwargs = {}
    n_in = len(args)
    kern = functools.partial(_gdn_kernel, rev=rev, has_s0=s0 is not None, finish=finish,
                             emit_state=emit_state, ntb=ntb)
    if prev is not None:
        args.append(prev)
        in_specs.append(pl.BlockSpec(memory_space=pl.ANY))
        kwargs["input_output_aliases"] = {n_in: 0}
        kern = functools.partial(_gdn_alias_kernel, n_in=n_in, inner=kern)
    res = pl.pallas_call(
        kern,
        grid=(nb, GDN_HEADS, ntb),
        in_specs=in_specs,
        out_specs=out_specs,
        out_shape=out_shape,
        scratch_shapes=[pltpu.VMEM((GDN_DK, GDN_DV), F32)],
        compiler_params=_cparams(("arbitrary", "arbitrary", "arbitrary")),
        name=f"gdn_{'bwd' if rev else 'fwd'}_{n}",
        **kwargs,
    )(*args)
    return res


def _gdn_alias_kernel(*refs, n_in, inner):
    inner(*(refs[:n_in] + refs[n_in + 1:]))


def _attn_kernel(*refs, lam_init, has_cache):
    if has_cache:
        q_ref, k_ref, v_ref, kc_ref, vc_ref, lam_ref, g_ref, o_ref = refs
    else:
        q_ref, k_ref, v_ref, lam_ref, g_ref, o_ref = refs
    lv = lam_ref[...]
    lam = (jnp.exp(jnp.sum(lv[0:1, :] * lv[1:2, :], axis=-1, keepdims=True))
           - jnp.exp(jnp.sum(lv[2:3, :] * lv[3:4, :], axis=-1, keepdims=True)) + lam_init)
    q = q_ref[...] * (DIFF_DH ** -0.5)
    lane = lax.broadcasted_iota(jnp.int32, q.shape, 1)
    qs = (jnp.where(lane < DIFF_DH, q, 0.0).astype(BF16), jnp.where(lane >= DIFF_DH, q, 0.0).astype(BF16))
    kb = k_ref[...].astype(BF16)
    kcb = kc_ref[...].astype(BF16) if has_cache else None
    probs = []
    for qh in qs:
        s = _dot_nt(qh, kb)
        m = jnp.max(s, axis=-1, keepdims=True)
        if has_cache:
            sc = _dot_nt(qh, kcb)
            m = jnp.maximum(m, jnp.max(sc, axis=-1, keepdims=True))
            ec = jnp.exp(sc - m)
        e = jnp.exp(s - m)
        den = jnp.sum(e, axis=-1, keepdims=True)
        if has_cache:
            den = den + jnp.sum(ec, axis=-1, keepdims=True)
        inv = 1.0 / den
        probs.append((e * inv, ec * inv if has_cache else None))
    a = probs[0][0] - lam * probs[1][0]
    o = _dot(a.astype(BF16), v_ref[...].astype(BF16))
    if has_cache:
        ac = probs[0][1] - lam * probs[1][1]
        o = o + _dot(ac.astype(BF16), vc_ref[...].astype(BF16))
    o = o * lax.rsqrt(jnp.mean(o * o, axis=-1, keepdims=True) + NORM_EPS) * g_ref[...] * (1.0 - lam_init)
    o_ref[...] = o.astype(o_ref.dtype)


def _attn_prompt(p_odd, lam_rows, norm_g, il, lam_init):
    n = SEQ
    return pl.pallas_call(
        functools.partial(_attn_kernel, lam_init=lam_init, has_cache=False),
        grid=(BATCH, DIFF_HEADS),
        in_specs=[
            pl.BlockSpec((n, LANES), lambda b, h: (b, h)),
            pl.BlockSpec((n, LANES), lambda b, h: (b, DIFF_HEADS + h)),
            pl.BlockSpec((n, LANES), lambda b, h: (b, 2 * DIFF_HEADS + h)),
            pl.BlockSpec((None, 8, LANES), lambda b, h: (il, 0, 0)),
            pl.BlockSpec((None, 1, DIFF_DV), lambda b, h: (il, 0, 0)),
        ],
        out_specs=pl.BlockSpec((n, LANES), lambda b, h: (b, h)),
        out_shape=jax.ShapeDtypeStruct((N_TOK, DIFF_WIDTH), BF16),
        compiler_params=_cparams(("arbitrary", "arbitrary")),
        name="attn_prompt",
    )(p_odd, p_odd, p_odd, lam_rows, norm_g)


def _attn_sample_kernel(*refs, lam_init):
    _attn_kernel(*(refs[:7] + refs[8:]), lam_init=lam_init, has_cache=True)


def _attn_sample(p_odd, cache_k, cache_v, lam_rows, norm_g, il, lam_init, o_prev):
    tq = TQ_ATT
    nq = DEC_SEQ // tq
    kblk = N_P // DEC_SEQ
    return pl.pallas_call(
        functools.partial(_attn_sample_kernel, lam_init=lam_init),
        grid=(DEC_BATCH, DIFF_HEADS, nq),
        in_specs=[
            pl.BlockSpec((tq, LANES), lambda b, h, i: (N_P // tq + b * nq + i, h)),
            pl.BlockSpec((DEC_SEQ, LANES), lambda b, h, i: (kblk + b, DIFF_HEADS + h)),
            pl.BlockSpec((DEC_SEQ, LANES), lambda b, h, i: (kblk + b, 2 * DIFF_HEADS + h)),
            pl.BlockSpec((None, None, PAST_LEN, LANES), lambda b, h, i: (b, il, 0, h)),
            pl.BlockSpec((None, None, PAST_LEN, LANES), lambda b, h, i: (b, il, 0, h)),
            pl.BlockSpec((None, 8, LANES), lambda b, h, i: (il, 0, 0)),
            pl.BlockSpec((None, 1, DIFF_DV), lambda b, h, i: (il, 0, 0)),
            pl.BlockSpec(memory_space=pl.ANY),
        ],
        out_specs=pl.BlockSpec((tq, LANES), lambda b, h, i: (N_P // tq + b * nq + i, h)),
        out_shape=jax.ShapeDtypeStruct((N_TOK, DIFF_WIDTH), BF16),
        input_output_aliases={7: 0},
        compiler_params=_cparams(("arbitrary", "arbitrary", "arbitrary")),
        name="attn_sample",
    )(p_odd, p_odd, p_odd, cache_k, cache_v, lam_rows, norm_g, o_prev)


def _outproj_kernel(*refs, n_parts, ln_row):
    parts = refs[:n_parts]
    ws = refs[n_parts:2 * n_parts]
    x_ref, mod_ref, lng_ref, lnb_ref, o_ref = refs[2 * n_parts:]
    y = _dot(parts[0][...], ws[0][...])
    for p_ref, w_ref in zip(parts[1:], ws[1:]):
        y = y + _dot(p_ref[...], w_ref[...])
    r = ALPHA * x_ref[...] + mod_ref[2:3, :] * y
    o_ref[...] = _layer_norm(r, lng_ref[ln_row:ln_row + 1, :], lnb_ref[ln_row:ln_row + 1, :])


def _outproj_ln(parts, w_all, il, x, mod_all, layer, ln_g, ln_b):
    tm = TM_LN
    n_parts = len(parts)
    in_specs = []
    for p in parts:
        in_specs.append(pl.BlockSpec((tm, p.shape[1]), lambda i: (i, 0)))
    off = 0
    for idx, p in enumerate(parts):
        wdt = p.shape[1]
        in_specs.append(pl.BlockSpec((None, wdt, D_MODEL), functools.partial(lambda i, o: (il, o, 0), o=off // wdt)))
        off += wdt
    in_specs += [
        pl.BlockSpec((tm, D_MODEL), lambda i: (i, 0)),
        pl.BlockSpec((None, None, 8, D_MODEL), lambda i: (layer, _group_of_row(i * tm), 0, 0)),
        pl.BlockSpec((2 * DEPTH, D_MODEL), lambda i: (0, 0)),
        pl.BlockSpec((2 * DEPTH, D_MODEL), lambda i: (0, 0)),
    ]
    return pl.pallas_call(
        functools.partial(_outproj_kernel, n_parts=n_parts, ln_row=2 * layer),
        grid=(N_TOK // tm,),
        in_specs=in_specs,
        out_specs=pl.BlockSpec((tm, D_MODEL), lambda i: (i, 0)),
        out_shape=jax.ShapeDtypeStruct((N_TOK, D_MODEL), F32),
        compiler_params=_cparams(("arbitrary",)),
        name="outproj_ln",
    )(*parts, *([w_all] * n_parts), x, mod_all, ln_g, ln_b)


def _ffn_dense_kernel(x_ref, mod_ref, lng_ref, lnb_ref, wg_ref, wu_ref, wd_ref, o_ref, h_scr, acc_scr, *,
                      ln_row, nf):
    j = pl.program_id(1)

    @pl.when(j == 0)
    def _():
        h_scr[...] = _modulated(x_ref, mod_ref, 3).astype(BF16)
        acc_scr[...] = jnp.zeros_like(acc_scr)

    hb = h_scr[...]
    g = _dot(hb, wg_ref[...])
    u = _dot(hb, wu_ref[...])
    a = (g * _sigmoid(g) * u).astype(BF16)
    acc_scr[...] += _dot(a, wd_ref[...])

    @pl.when(j == nf - 1)
    def _():
        r = ALPHA * x_ref[...] + mod_ref[5:6, :] * acc_scr[...]
        o_ref[...] = _layer_norm(r, lng_ref[ln_row:ln_row + 1, :], lnb_ref[ln_row:ln_row + 1, :])


def _ffn_dense(x, mod_all, layer, ln_g, ln_b, w_gu, w_down, il):
    tm, tf = TM_FFN, TF_FFN
    nf = FFN_DIM // tf
    return pl.pallas_call(
        functools.partial(_ffn_dense_kernel, ln_row=2 * layer + 1, nf=nf),
        grid=(N_TOK // tm, nf),
        in_specs=[
            pl.BlockSpec((tm, D_MODEL), lambda i, j: (i, 0)),
            pl.BlockSpec((None, None, 8, D_MODEL), lambda i, j: (layer, _group_of_row(i * tm), 0, 0)),
            pl.BlockSpec((2 * DEPTH, D_MODEL), lambda i, j: (0, 0)),
            pl.BlockSpec((2 * DEPTH, D_MODEL), lambda i, j: (0, 0)),
            pl.BlockSpec((None, D_MODEL, tf), lambda i, j: (il, 0, j)),
            pl.BlockSpec((None, D_MODEL, tf), lambda i, j: (il, 0, nf + j)),
            pl.BlockSpec((None, tf, D_MODEL), lambda i, j: (il, j, 0)),
        ],
        out_specs=pl.BlockSpec((tm, D_MODEL), lambda i, j: (i, 0)),
        out_shape=jax.ShapeDtypeStruct((N_TOK, D_MODEL), F32),
        scratch_shapes=[pltpu.VMEM((tm, D_MODEL), BF16), pltpu.VMEM((tm, D_MODEL), F32)],
        compiler_params=_cparams(("arbitrary", "arbitrary")),
        name="ffn_dense",
    )(x, mod_all, ln_g, ln_b, w_gu, w_gu, w_down)


def _router_kernel(x_ref, mod_ref, wr_ref, h_ref, idx_ref, gate_ref):
    h = _modulated(x_ref, mod_ref, 3)
    h_ref[...] = h
    logits = _dot(h, wr_ref[...], HI)
    lane_i = lax.broadcasted_iota(jnp.int32, logits.shape, 1)
    lane = lane_i.astype(F32)
    neg = -1e30
    logits = jnp.where(lane_i < N_EXPERTS, logits, neg)
    m1 = jnp.max(logits, axis=-1, keepdims=True)
    i1 = jnp.min(jnp.where(logits == m1, lane, float(LANES)), axis=-1, keepdims=True)
    rest = jnp.where(lane == i1, neg, logits)
    m2 = jnp.max(rest, axis=-1, keepdims=True)
    i2 = jnp.min(jnp.where(rest == m2, lane, float(LANES)), axis=-1, keepdims=True)
    e = jnp.exp(m2 - m1)
    g1 = 1.0 / (1.0 + e)
    g2 = e / (1.0 + e)
    idx_ref[...] = jnp.where(lane_i == 0, i1, jnp.where(lane_i == 1, i2, 0.0)).astype(jnp.int32)
    gate_ref[...] = jnp.where(lane_i == 0, g1, jnp.where(lane_i == 1, g2, 0.0))


def _router(x, mod_all, layer, router_w_pad, il):
    tm = TM_TOK
    return pl.pallas_call(
        _router_kernel,
        grid=(N_TOK // tm,),
        in_specs=[
            pl.BlockSpec((tm, D_MODEL), lambda i: (i, 0)),
            pl.BlockSpec((None, None, 8, D_MODEL), lambda i: (layer, _group_of_row(i * tm), 0, 0)),
            pl.BlockSpec((None, D_MODEL, LANES), lambda i: (il, 0, 0)),
        ],
        out_specs=[
            pl.BlockSpec((tm, D_MODEL), lambda i: (i, 0)),
            pl.BlockSpec((tm, LANES), lambda i: (i, 0)),
            pl.BlockSpec((tm, LANES), lambda i: (i, 0)),
        ],
        out_shape=[
            jax.ShapeDtypeStruct((N_TOK, D_MODEL), F32),
            jax.ShapeDtypeStruct((N_TOK, LANES), jnp.int32),
            jax.ShapeDtypeStruct((N_TOK, LANES), F32),
        ],
        compiler_params=_cparams(("arbitrary",)),
        name="router",
    )(x, mod_all, router_w_pad)


def _routing_plan(idx):
    tm = TM_MOE
    e_flat = idx[:, :TOP_K].reshape(-1)
    onehot = (e_flat[:, None] == jnp.arange(N_EXPERTS, dtype=jnp.int32)[None, :]).astype(jnp.int32)
    csum = jnp.cumsum(onehot, axis=0)
    rank = jnp.take_along_axis(csum, e_flat[:, None], axis=1)[:, 0] - 1
    counts = csum[-1]
    tiles_per = (counts + tm - 1) // tm
    tile_end = jnp.cumsum(tiles_per)
    tile_start = tile_end - tiles_per
    pos = tile_start[e_flat] * tm + rank
    n_used = tile_end[-1]
    tile_ids = jnp.arange(MOE_TILES, dtype=jnp.int32)
    tile_expert = jnp.sum((tile_ids[:, None] >= tile_end[None, :]).astype(jnp.int32), axis=1)
    last_expert = jnp.max(jnp.where(counts > 0, jnp.arange(N_EXPERTS, dtype=jnp.int32), 0))
    tile_expert = jnp.minimum(tile_expert, last_expert).astype(jnp.int32)
    tok = jnp.arange(TOP_K * N_TOK, dtype=jnp.int32) // TOP_K
    row_tok = jnp.zeros((MOE_ROWS,), jnp.int32).at[pos].set(tok, unique_indices=True)
    return pos.astype(jnp.int32), row_tok, tile_expert, n_used.astype(jnp.int32).reshape(1)


def _ffn_moe_kernel(texp_ref, nused_ref, rowtok_ref, h_hbm, wg_ref, wu_ref, wd_ref, o_ref,
                    xbuf, h_scr, acc_scr, sem, *, tm, nf):
    i = pl.program_id(0)
    j = pl.program_id(1)
    used = i < nused_ref[0]

    def row_copy(r):
        return pltpu.make_async_copy(h_hbm.at[pl.ds(rowtok_ref[i * tm + r], 1), :], xbuf.at[pl.ds(r, 1), :], sem)

    @pl.when(jnp.logical_and(used, j == 0))
    def _():
        def start(r, carry):
            row_copy(r).start()
            return carry

        lax.fori_loop(0, tm, start, 0)

        def wait(r, carry):
            row_copy(r).wait()
            return carry

        lax.fori_loop(0, tm, wait, 0)
        h_scr[...] = xbuf[...].astype(BF16)
        acc_scr[...] = jnp.zeros_like(acc_scr)

    @pl.when(used)
    def _():
        hb = h_scr[...]
        g = _dot(hb, wg_ref[...].astype(BF16))
        u = _dot(hb, wu_ref[...].astype(BF16))
        a = (g * _sigmoid(g) * u).astype(BF16)
        acc_scr[...] += _dot(a, wd_ref[...].astype(BF16))

    @pl.when(jnp.logical_and(used, j == nf - 1))
    def _():
        o_ref[...] = acc_scr[...]

    @pl.when(jnp.logical_and(jnp.logical_not(used), j == nf - 1))
    def _():
        o_ref[...] = jnp.zeros_like(o_ref)


def _ffn_moe(h, plan, w_gu, w_down, il):
    _, row_tok, tile_expert, n_used = plan
    tm, tf = TM_MOE, TF_FFN
    nf = EXPERT_DIM // tf

    def wmap(off):
        def f(i, j, texp, nused, rowtok):
            live = i < nused[0]
            ii = jnp.minimum(i, nused[0] - 1)
            return (il, texp[ii], 0, off + jnp.where(live, j, nf - 1))
        return f

    def dmap(i, j, texp, nused, rowtok):
        live = i < nused[0]
        ii = jnp.minimum(i, nused[0] - 1)
        return (il, texp[ii], jnp.where(live, j, nf - 1), 0)

    grid_spec = pltpu.PrefetchScalarGridSpec(
        num_scalar_prefetch=3,
        grid=(MOE_TILES, nf),
        in_specs=[
            pl.BlockSpec(memory_space=pl.ANY),
            pl.BlockSpec((None, None, D_MODEL, tf), wmap(0)),
            pl.BlockSpec((None, None, D_MODEL, tf), wmap(nf)),
            pl.BlockSpec((None, None, tf, D_MODEL), dmap),
        ],
        out_specs=pl.BlockSpec((tm, D_MODEL), lambda i, j, *_: (i, 0)),
        scratch_shapes=[
            pltpu.VMEM((tm, D_MODEL), F32),
            pltpu.VMEM((tm, D_MODEL), BF16),
            pltpu.VMEM((tm, D_MODEL), F32),
            pltpu.SemaphoreType.DMA(()),
        ],
    )
    return pl.pallas_call(
        functools.partial(_ffn_moe_kernel, tm=tm, nf=nf),
        grid_spec=grid_spec,
        out_shape=jax.ShapeDtypeStruct((MOE_ROWS, D_MODEL), F32),
        compiler_params=_cparams(("arbitrary", "arbitrary")),
        name="ffn_moe",
    )(tile_expert, n_used, row_tok, h, w_gu, w_gu, w_down)


def _combine_kernel(pos_ref, y_hbm, gate_ref, x_ref, mod_ref, lng_ref, lnb_ref, o_ref, ybuf, sem, *, tm, ln_row):
    i = pl.program_id(0)

    def row_copy(r, k):
        return pltpu.make_async_copy(y_hbm.at[pl.ds(pos_ref[TOP_K * (i * tm + r) + k], 1), :],
                                     ybuf.at[k, pl.ds(r, 1), :], sem)

    def start(r, carry):
        for k in range(TOP_K):
            row_copy(r, k).start()
        return carry

    lax.fori_loop(0, tm, start, 0)

    def wait(r, carry):
        for k in range(TOP_K):
            row_copy(r, k).wait()
        return carry

    lax.fori_loop(0, tm, wait, 0)
    gates = gate_ref[...]
    f = gates[:, 0:1] * ybuf[0] + gates[:, 1:2] * ybuf[1]
    r = ALPHA * x_ref[...] + mod_ref[5:6, :] * f
    o_ref[...] = _layer_norm(r, lng_ref[ln_row:ln_row + 1, :], lnb_ref[ln_row:ln_row + 1, :])


def _moe_combine(y_rows, pos, gates, x, mod_all, layer, ln_g, ln_b):
    tm = TM_TOK
    grid_spec = pltpu.PrefetchScalarGridSpec(
        num_scalar_prefetch=1,
        grid=(N_TOK // tm,),
        in_specs=[
            pl.BlockSpec(memory_space=pl.ANY),
            pl.BlockSpec((tm, LANES), lambda i, *_: (i, 0)),
            pl.BlockSpec((tm, D_MODEL), lambda i, *_: (i, 0)),
            pl.BlockSpec((None, None, 8, D_MODEL), lambda i, *_: (layer, _group_of_row(i * tm), 0, 0)),
            pl.BlockSpec((2 * DEPTH, D_MODEL), lambda i, *_: (0, 0)),
            pl.BlockSpec((2 * DEPTH, D_MODEL), lambda i, *_: (0, 0)),
        ],
        out_specs=pl.BlockSpec((tm, D_MODEL), lambda i, *_: (i, 0)),
        scratch_shapes=[pltpu.VMEM((TOP_K, tm, D_MODEL), F32), pltpu.SemaphoreType.DMA(())],
    )
    return pl.pallas_call(
        functools.partial(_combine_kernel, tm=tm, ln_row=2 * layer + 1),
        grid_spec=grid_spec,
        out_shape=jax.ShapeDtypeStruct((N_TOK, D_MODEL), F32),
        compiler_params=_cparams(("arbitrary",)),
        name="moe_combine",
    )(pos, y_rows, gates, x, mod_all, ln_g, ln_b)


def _pad_lanes(a, lane0=0):
    width = a.shape[-1]
    pads = [(0, 0)] * (a.ndim - 1) + [(lane0, LANES - lane0 - width)]
    return jnp.pad(a, pads)


def kernel(x_prompt, x_sample, state_gdn, cache_diff_k, cache_diff_v, c, c_ctx, w_ada, b_ada, ln_g, ln_b,
           w_in_even, gdn_conv_w, gdn_a_log, gdn_dt_bias, gdn_norm_g, w_out_even, w_in_odd,
           lam_q1, lam_k1, lam_q2, lam_k2, diff_norm_g, w_out_odd, ffn_w_gu, ffn_w_down,
           router_w, moe_w_gu, moe_w_down):
    x = jnp.concatenate([x_prompt.reshape(N_P, D_MODEL), x_sample.reshape(N_S, D_MODEL)], axis=0)

    cond8 = jnp.concatenate([c_ctx[None, :], c, jnp.zeros((8 - N_GROUPS, D_MODEL), F32)], axis=0)
    mod = _ada_modulation(cond8, w_ada, b_ada)
    mod_all = jnp.pad(mod.reshape(DEPTH, 8, 6, D_MODEL)[:, :N_GROUPS], ((0, 0), (0, 0), (0, 2), (0, 0)))

    ln_g2 = ln_g.reshape(2 * DEPTH, D_MODEL)
    ln_b2 = ln_b.reshape(2 * DEPTH, D_MODEL)

    w_even_main = w_in_even[:, :, :EVEN_MAIN].astype(BF16)
    w_even_tail = _pad_lanes(w_in_even[:, :, EVEN_MAIN:]).astype(BF16)
    w_even_tail_t = jnp.swapaxes(w_even_tail, 1, 2)
    w_out_even_b = w_out_even.astype(BF16)
    w_in_odd_b = w_in_odd.astype(BF16)
    w_out_odd_b = w_out_odd.astype(BF16)
    ffn_w_gu_b = ffn_w_gu.astype(BF16)
    ffn_w_down_b = ffn_w_down.astype(BF16)
    router_w_pad = _pad_lanes(router_w)

    a_row = _pad_lanes(gdn_a_log.reshape(N_EVEN, 1, 2 * GDN_HEADS), 2 * GDN_HEADS)
    dt_row = _pad_lanes(gdn_dt_bias.reshape(N_EVEN, 1, 2 * GDN_HEADS), 2 * GDN_HEADS)
    gate_params = (a_row, dt_row, jnp.swapaxes(a_row, 1, 2), jnp.swapaxes(dt_row, 1, 2))
    gdn_norm3 = gdn_norm_g.reshape(N_EVEN, 1, GDN_DV)

    rope = _rope_tables()
    lam_rows = jnp.pad(jnp.stack([lam_q1, lam_k1, lam_q2, lam_k2], axis=1),
                       ((0, 0), (0, 4), (0, LANES - DIFF_DH)))
    diff_norm3 = diff_norm_g.reshape(N_ODD, 1, DIFF_DV)
    cache_k = cache_diff_k.reshape(DEC_BATCH, N_ODD, PAST_LEN, DIFF_WIDTH)
    cache_v = cache_diff_v.reshape(DEC_BATCH, N_ODD, PAST_LEN, DIFF_WIDTH)

    gdn_states, ctx_keys, ctx_vals = [], [], []
    for layer in range(DEPTH):
        il = layer // 2
        if layer % 2 == 0:
            p_main, tail, tail_t = _inproj_even(x, mod_all, layer, w_even_main, w_even_tail, w_even_tail_t, il)
            y_a = _fnet(p_main, 0, BATCH, SEQ)
            y_a = _fnet(p_main, N_P, DEC_BATCH, DEC_SEQ, y_prev=y_a)
            qkv = _gdn_prep(p_main, gdn_conv_w, il, 0, BATCH, SEQ)
            qkv = _gdn_prep(p_main, gdn_conv_w, il, N_P, DEC_BATCH, DEC_SEQ, prev=qkv)
            common = (qkv, tail, tail_t, gate_params, il)
            o_f, s_f = _gdn_scan(*common, 0, BATCH, SEQ, rev=False, emit_state=True)
            (o_f,) = _gdn_scan(*common, N_P, DEC_BATCH, DEC_SEQ, rev=False, s0=state_gdn, prev=o_f)
            y_b, s_b = _gdn_scan(*common, 0, BATCH, SEQ, rev=True, o_fwd=o_f, z_src=p_main,
                                 norm_g=gdn_norm3, emit_state=True)
            (y_b,) = _gdn_scan(*common, N_P, DEC_BATCH, DEC_SEQ, rev=True, s0=state_gdn, o_fwd=o_f,
                               z_src=p_main, norm_g=gdn_norm3, prev=y_b)
            gdn_states.append(jnp.stack([s_f, s_b], axis=1))
            x = _outproj_ln([y_a, y_b], w_out_even_b, il, x, mod_all, layer, ln_g2, ln_b2)
            x = _ffn_dense(x, mod_all, layer, ln_g2, ln_b2, ffn_w_gu_b, ffn_w_down_b, il)
        else:
            lam_init = 0.8 - 0.6 * math.exp(-0.3 * layer)
            p_odd = _inproj_odd(x, mod_all, layer, w_in_odd_b, il, rope)
            ctx_keys.append(p_odd[:N_P, DIFF_WIDTH:2 * DIFF_WIDTH].reshape(BATCH, SEQ, 2 * DIFF_HEADS, DIFF_DH))
            ctx_vals.append(p_odd[:N_P, 2 * DIFF_WIDTH:].reshape(BATCH, SEQ, DIFF_HEADS, DIFF_DV))
            o = _attn_prompt(p_odd, lam_rows, diff_norm3, il, lam_init)
            o = _attn_sample(p_odd, cache_k, cache_v, lam_rows, diff_norm3, il, lam_init, o)
            x = _outproj_ln([o], w_out_odd_b, il, x, mod_all, layer, ln_g2, ln_b2)
            h, idx, gates = _router(x, mod_all, layer, router_w_pad, il)
            plan = _routing_plan(idx)
            y_rows = _ffn_moe(h, plan, moe_w_gu, moe_w_down, il)
            x = _moe_combine(y_rows, plan[0], gates, x, mod_all, layer, ln_g2, ln_b2)

    y_prompt = x[:N_P].reshape(BATCH, SEQ, D_MODEL)
    y_sample = x[N_P:].reshape(DEC_BATCH, DEC_SEQ, D_MODEL)
    new_state_gdn = jnp.stack(gdn_states, axis=1)
    new_cache_k = jnp.stack(ctx_keys, axis=1)
    new_cache_v = jnp.stack(ctx_vals, axis=1)
    return (y_prompt, y_sample, new_state_gdn, new_cache_k, new_cache_v)
```

```python
import functools
import math

import numpy as np
import jax
import jax.numpy as jnp
from jax import lax
from jax.experimental import pallas as pl
from jax.experimental.pallas import tpu as pltpu

F32 = jnp.float32
BF16 = jnp.bfloat16
HI = lax.Precision.HIGHEST

D_MODEL = 2048
BATCH = 16
SEQ = 256
DEPTH = 4
DEC_BATCH = 4
DEC_SEQ = 2048
PAST_LEN = 256
GRID_W = 64
N_EVEN = (DEPTH + 1) // 2
N_ODD = DEPTH // 2
FNET_GROUPS = 4
FNET_GROUP_W = D_MODEL // 8
FNET_WIDTH = FNET_GROUPS * FNET_GROUP_W
GDN_DK = 128
GDN_DV = 128
GDN_HEADS = (D_MODEL // 2) // GDN_DV
GDN_WIDTH = GDN_HEADS * GDN_DV
GDN_QKV = GDN_HEADS * (2 * GDN_DK + GDN_DV)
GDN_CONV = 3
EVEN_MAIN = FNET_WIDTH + GDN_QKV + GDN_WIDTH
DIFF_DH = 64
DIFF_DV = 2 * DIFF_DH
DIFF_HEADS = D_MODEL // DIFF_DV
DIFF_WIDTH = DIFF_HEADS * DIFF_DV
ROPE_BASE = 10000.0
FFN_DIM = 5632
N_EXPERTS = 8
TOP_K = 2
EXPERT_DIM = 7168
ALPHA = (2 * DEPTH) ** 0.25
LN_EPS = 1e-5
NORM_EPS = 1e-6

N_P = BATCH * SEQ
N_S = DEC_BATCH * DEC_SEQ
N_TOK = N_P + N_S
N_GROUPS = 1 + DEC_BATCH

LANES = 128
VMEM_LIMIT = 56 * 1024 * 1024
GDN_CHUNK = 64
GDN_TB = 256
GDN_HP = 4
TM_PROJ = 1024
TN_PROJ = 1024
TM_LN = 512
TM_FFN = 512
TF_FFN = 512
TM_MOE = 1024
TF_MOE = 512
TM_TOK = 256
TQ_ATT = 256
TI_FNET = 512
MOE_ROWS = TOP_K * N_TOK + N_EXPERTS * TM_MOE
MOE_TILES = MOE_ROWS // TM_MOE


def _cparams(sem):
    return pltpu.CompilerParams(dimension_semantics=sem, vmem_limit_bytes=VMEM_LIMIT)


def _group_of_row(r0):
    return jnp.where(r0 < N_P, 0, 1 + (r0 - N_P) // DEC_SEQ)


def _sigmoid(x):
    return 1.0 / (1.0 + jnp.exp(-x))


def _softplus(x):
    return jnp.maximum(x, 0.0) + jnp.log1p(jnp.exp(-jnp.abs(x)))


def _dot(a, b, precision=None):
    return jnp.dot(a, b, preferred_element_type=F32, precision=precision)


def _dot_nt(a, b):
    return lax.dot_general(a, b, (((1,), (1,)), ((), ())), preferred_element_type=F32)


def _dot_tn(a, b):
    return lax.dot_general(a, b, (((0,), (0,)), ((), ())), preferred_element_type=F32)


def _layer_norm(r, g, b):
    mu = jnp.mean(r, axis=-1, keepdims=True)
    d = r - mu
    var = jnp.mean(d * d, axis=-1, keepdims=True)
    return d * lax.rsqrt(var + LN_EPS) * g + b


def _ada_kernel(cond_ref, w_ref, b_ref, o_ref):
    c = cond_ref[...]
    s = c * _sigmoid(c)
    o_ref[...] = _dot(s, w_ref[...], HI) + b_ref[...]


def _ada_modulation(cond8, w_ada, b_ada):
    tn = 1024
    return pl.pallas_call(
        _ada_kernel,
        grid=(DEPTH, 6 * D_MODEL // tn),
        in_specs=[
            pl.BlockSpec((8, D_MODEL), lambda l, j: (0, 0)),
            pl.BlockSpec((None, D_MODEL, tn), lambda l, j: (l, 0, j)),
            pl.BlockSpec((None, 1, tn), lambda l, j: (l, 0, j)),
        ],
        out_specs=pl.BlockSpec((None, 8, tn), lambda l, j: (l, 0, j)),
        out_shape=jax.ShapeDtypeStruct((DEPTH, 8, 6 * D_MODEL), F32),
        compiler_params=_cparams(("arbitrary", "arbitrary")),
        name="ada_modulation",
    )(cond8, w_ada, b_ada.reshape(DEPTH, 1, 6 * D_MODEL))


def _modulated(x_ref, mod_ref, shift_row):
    m = mod_ref[...]
    return x_ref[...] * (1.0 + m[shift_row + 1:shift_row + 2, :]) + m[shift_row:shift_row + 1, :]


def _inproj_even_kernel(x_ref, mod_ref, w_ref, wt_ref, wtt_ref, o_ref, ot_ref, ott_ref, h_scr):
    @pl.when(pl.program_id(1) == 0)
    def _():
        hb = _modulated(x_ref, mod_ref, 0).astype(BF16)
        h_scr[...] = hb
        ot_ref[...] = _dot(hb, wt_ref[...])
        ott_ref[...] = _dot_nt(wtt_ref[...], hb)

    o_ref[...] = _dot(h_scr[...], w_ref[...])


def _inproj_even(x, mod_all, layer, w_main, w_tail, w_tail_t, il):
    tm, tn = TM_PROJ, TN_PROJ
    return pl.pallas_call(
        _inproj_even_kernel,
        grid=(N_TOK // tm, EVEN_MAIN // tn),
        in_specs=[
            pl.BlockSpec((tm, D_MODEL), lambda i, j: (i, 0)),
            pl.BlockSpec((None, None, 8, D_MODEL), lambda i, j: (layer, _group_of_row(i * tm), 0, 0)),
            pl.BlockSpec((None, D_MODEL, tn), lambda i, j: (il, 0, j)),
            pl.BlockSpec((None, D_MODEL, LANES), lambda i, j: (il, 0, 0)),
            pl.BlockSpec((None, LANES, D_MODEL), lambda i, j: (il, 0, 0)),
        ],
        out_specs=[
            pl.BlockSpec((tm, tn), lambda i, j: (i, j)),
            pl.BlockSpec((tm, LANES), lambda i, j: (i, 0)),
            pl.BlockSpec((LANES, tm), lambda i, j: (0, i)),
        ],
        out_shape=[
            jax.ShapeDtypeStruct((N_TOK, EVEN_MAIN), F32),
            jax.ShapeDtypeStruct((N_TOK, LANES), F32),
            jax.ShapeDtypeStruct((LANES, N_TOK), F32),
        ],
        scratch_shapes=[pltpu.VMEM((tm, D_MODEL), BF16)],
        compiler_params=_cparams(("arbitrary", "arbitrary")),
        name="inproj_even",
    )(x, mod_all, w_main, w_tail, w_tail_t)


def _inproj_odd_kernel(x_ref, mod_ref, w_ref, cos_ref, sa_ref, sb_ref, o_ref, h_scr, *, tm, tn):
    i = pl.program_id(0)
    j = pl.program_id(1)

    @pl.when(j == 0)
    def _():
        h_scr[...] = _modulated(x_ref, mod_ref, 0).astype(BF16)

    y = _dot(h_scr[...], w_ref[...])
    is_rope = jnp.logical_and(i * tm >= N_P, j * tn < 2 * DIFF_WIDTH)

    @pl.when(is_rope)
    def _():
        c = cos_ref[...]
        sa = sa_ref[...]
        sb = sb_ref[...]
        for cb in range(tn // LANES):
            yc = y[:, cb * LANES:(cb + 1) * LANES]
            o_ref[:, cb * LANES:(cb + 1) * LANES] = (
                yc * c + pltpu.roll(yc, 16, 1) * sa + pltpu.roll(yc, LANES - 16, 1) * sb)

    @pl.when(jnp.logical_not(is_rope))
    def _():
        o_ref[...] = y


def _inproj_odd(x, mod_all, layer, w_in, il, rope):
    tm, tn = TM_PROJ, TN_PROJ

    def rope_map(i, j):
        r0 = i * tm
        return (jnp.where(r0 >= N_P, ((r0 - N_P) % DEC_SEQ) // tm, 0), 0)

    return pl.pallas_call(
        functools.partial(_inproj_odd_kernel, tm=tm, tn=tn),
        grid=(N_TOK // tm, 3 * DIFF_WIDTH // tn),
        in_specs=[
            pl.BlockSpec((tm, D_MODEL), lambda i, j: (i, 0)),
            pl.BlockSpec((None, None, 8, D_MODEL), lambda i, j: (layer, _group_of_row(i * tm), 0, 0)),
            pl.BlockSpec((None, D_MODEL, tn), lambda i, j: (il, 0, j)),
            pl.BlockSpec((tm, LANES), rope_map),
            pl.BlockSpec((tm, LANES), rope_map),
            pl.BlockSpec((tm, LANES), rope_map),
        ],
        out_specs=pl.BlockSpec((tm, tn), lambda i, j: (i, j)),
        out_shape=jax.ShapeDtypeStruct((N_TOK, 3 * DIFF_WIDTH), F32),
        scratch_shapes=[pltpu.VMEM((tm, D_MODEL), BF16)],
        compiler_params=_cparams(("arbitrary", "arbitrary")),
        name="inproj_odd",
    )(x, mod_all, w_in, *rope)


def _rope_tables():
    t = np.arange(DEC_SEQ)
    pos = np.stack([t // GRID_W, t % GRID_W], axis=1).astype(np.float32)
    axis_dim = DIFF_DH // 2
    half = axis_dim // 2
    inv_freq = (1.0 / (ROPE_BASE ** (np.arange(0, axis_dim, 2, dtype=np.float32) / axis_dim))).astype(np.float32)
    lane = np.arange(LANES)
    axis = (lane % DIFF_DH) // axis_dim
    within = lane % axis_dim
    ang = (pos[:, axis] * inv_freq[within % half][None, :]).astype(np.float32)
    cos = np.cos(ang).astype(np.float32)
    sin = np.sin(ang).astype(np.float32)
    second = (within >= half)[None, :]
    sa = np.where(second, sin, 0.0).astype(np.float32)
    sb = np.where(second, 0.0, -sin).astype(np.float32)
    return jnp.asarray(cos), jnp.asarray(sa), jnp.asarray(sb)


def _dft_tables(n):
    def cs(m):
        jk = (np.arange(m)[:, None] * np.arange(m)[None, :]) % m
        ang = 2.0 * np.pi * jk.astype(np.float64) / m
        return np.cos(ang), np.sin(ang)

    cc, sc = cs(FNET_GROUP_W)
    cn, sn = cs(n)
    chan = np.concatenate([cc, sc], axis=1)
    posn = np.concatenate([cn, -sn], axis=1)
    return jnp.asarray(chan, dtype=BF16), jnp.asarray(posn, dtype=BF16)


def _fnet_kernel(a_ref, chan_ref, posn_ref, o_ref, t_scr, *, n):
    @pl.when(pl.program_id(1) == 0)
    def _():
        a = a_ref[...].astype(BF16)
        w = FNET_GROUP_W
        for g in range(FNET_GROUPS):
            t = _dot(a[:, g * w:(g + 1) * w], chan_ref[...])
            t_scr[0:n, g * w:(g + 1) * w] = t[:, :w].astype(BF16)
            t_scr[n:2 * n, g * w:(g + 1) * w] = t[:, w:].astype(BF16)

    y = _dot(posn_ref[...], t_scr[...])
    o_ref[...] = (y * ((n * FNET_GROUP_W) ** -0.5)).astype(BF16)


def _fnet(p_main, row0, nb, n, y_prev=None):
    ti = min(TI_FNET, n)
    nt = n // ti
    chan, posn = _dft_tables(n)
    out_shape = jax.ShapeDtypeStruct((N_TOK, FNET_WIDTH), BF16)
    args = [p_main, chan, posn]
    in_specs = [
        pl.BlockSpec((n, FNET_WIDTH), lambda b, i: (row0 // n + b, 0)),
        pl.BlockSpec((FNET_GROUP_W, 2 * FNET_GROUP_W), lambda b, i: (0, 0)),
        pl.BlockSpec((ti, 2 * n), lambda b, i: (i, 0)),
    ]
    kwargs = {}
    kern = functools.partial(_fnet_kernel, n=n)
    if y_prev is not None:
        args.append(y_prev)
        in_specs.append(pl.BlockSpec(memory_space=pl.ANY))
        kwargs["input_output_aliases"] = {3: 0}
        kern = functools.partial(_fnet_alias_kernel, n=n)
    return pl.pallas_call(
        kern,
        grid=(nb, nt),
        in_specs=in_specs,
        out_specs=pl.BlockSpec((ti, FNET_WIDTH), lambda b, i: (row0 // ti + b * nt + i, 0)),
        out_shape=out_shape,
        scratch_shapes=[pltpu.VMEM((2 * n, FNET_WIDTH), BF16)],
        compiler_params=_cparams(("arbitrary", "arbitrary")),
        name=f"fnet_{n}",
        **kwargs,
    )(*args)


def _fnet_alias_kernel(a_ref, chan_ref, posn_ref, prev_ref, o_ref, t_scr, *, n):
    del prev_ref
    _fnet_kernel(a_ref, chan_ref, posn_ref, o_ref, t_scr, n=n)


def _gdn_prep_kernel(x_ref, w_ref, o_ref):
    j = pl.program_id(1)
    x = x_ref[...]
    n = x.shape[0]
    row = lax.broadcasted_iota(jnp.int32, x.shape, 0)
    x_prev = jnp.where(row == 0, 0.0, pltpu.roll(x, 1, 0))
    x_next = jnp.where(row == n - 1, 0.0, pltpu.roll(x, n - 1, 0))
    w = w_ref[...]
    y = x_prev * w[0:1, :] + x * w[1:2, :] + x_next * w[2:3, :]
    y = y * _sigmoid(y)
    inv = lax.rsqrt(jnp.sum(y * y, axis=-1, keepdims=True) + NORM_EPS)
    scale = jnp.where(j < GDN_HEADS, inv * (GDN_DK ** -0.5), jnp.where(j < 2 * GDN_HEADS, inv, 1.0))
    o_ref[...] = y * scale


def _gdn_prep(p_main, conv_w, il, row0, nb, n, prev=None):
    col0 = FNET_WIDTH // LANES
    args = [p_main, conv_w]
    in_specs = [
        pl.BlockSpec((n, LANES), lambda b, j: (row0 // n + b, col0 + j)),
        pl.BlockSpec((None, GDN_CONV, LANES), lambda b, j: (il, 0, j)),
    ]
    kwargs = {}
    kern = _gdn_prep_kernel
    if prev is not None:
        args.append(prev)
        in_specs.append(pl.BlockSpec(memory_space=pl.ANY))
        kwargs["input_output_aliases"] = {2: 0}
        kern = _gdn_prep_alias_kernel
    return pl.pallas_call(
        kern,
        grid=(nb, GDN_QKV // LANES),
        in_specs=in_specs,
        out_specs=pl.BlockSpec((n, LANES), lambda b, j: (row0 // n + b, j)),
        out_shape=jax.ShapeDtypeStruct((N_TOK, GDN_QKV), F32),
        compiler_params=_cparams(("arbitrary", "arbitrary")),
        name=f"gdn_prep_{n}",
        **kwargs,
    )(*args)


def _gdn_prep_alias_kernel(x_ref, w_ref, prev_ref, o_ref):
    del prev_ref
    _gdn_prep_kernel(x_ref, w_ref, o_ref)


def _split2(a):
    hi = a.astype(BF16)
    lo = (a - hi.astype(F32)).astype(BF16)
    return hi, lo


def _split3(a):
    hi = a.astype(BF16)
    r = a - hi.astype(F32)
    mid = r.astype(BF16)
    lo = (r - mid.astype(F32)).astype(BF16)
    return hi, mid, lo


def _dot_split(a, b):
    ah, al = _split2(a)
    bh, bl = _split2(b)
    return _dot(jnp.concatenate([ah, al, ah], axis=1), jnp.concatenate([bh, bh, bl], axis=0))


def _gdn_kernel(*refs, rev, has_s0, finish, emit_state, ntb):
    it = iter(refs)
    q_ref, k_ref, v_ref, tail_ref, tailt_ref = next(it), next(it), next(it), next(it), next(it)
    arow_ref, dtrow_ref, acol_ref, dtcol_ref = next(it), next(it), next(it), next(it)
    s0_ref = next(it) if has_s0 else None
    if finish:
        of_ref, z_ref, ng_ref = next(it), next(it), next(it)
    o_ref = next(it)
    sfin_ref = next(it) if emit_state else None
    s_scr = next(it)

    hg = pl.program_id(1)
    tb = pl.program_id(2)
    d = 1 if rev else 0
    c = GDN_CHUNK
    ncb = GDN_TB // c
    hp = GDN_HP

    @pl.when(tb == 0)
    def _():
        if has_s0:
            s_scr[...] = s0_ref[...]
        else:
            s_scr[...] = jnp.zeros_like(s_scr)

    tail = tail_ref[...]
    lane = lax.broadcasted_iota(jnp.int32, tail.shape, 1)
    g_all = -jnp.exp(arow_ref[...]) * _softplus(tail + dtrow_ref[...])
    beta_all = _sigmoid(tail)
    tail_t = tailt_ref[...]
    sub = lax.broadcasted_iota(jnp.int32, tail_t.shape, 0)
    g_all_t = -jnp.exp(acol_ref[...]) * _softplus(tail_t + dtcol_ref[...])

    ri = lax.broadcasted_iota(jnp.int32, (c, c), 0)
    ci = lax.broadcasted_iota(jnp.int32, (c, c), 1)
    eye = (ri == ci).astype(F32)
    if rev:
        m_incl = ri <= ci
        m_strict = ri < ci
    else:
        m_incl = ri >= ci
        m_strict = ri > ci
    cum_col = m_incl.astype(BF16)
    cum_row = ((ri >= ci) if rev else (ri <= ci)).astype(BF16)
    cum_col3 = jnp.concatenate([cum_col] * 3, axis=1)
    cum_row3 = jnp.concatenate([cum_row] * 3, axis=0)

    chunk_order = list(range(ncb - 1, -1, -1)) if rev else list(range(ncb))

    units = [(hh, cc) for hh in range(hp) for cc in chunk_order]

    def cols_of(hh):
        return slice(hh * LANES, (hh + 1) * LANES)

    def rows_of(cc):
        return slice(cc * c, (cc + 1) * c)

    heads = {}
    for hh in range(hp):
        gcol = 2 * GDN_HEADS + d * GDN_HEADS + hg * hp + hh
        bcol = d * GDN_HEADS + hg * hp + hh
        heads[hh] = (
            jnp.sum(jnp.where(lane == gcol, g_all, 0.0), axis=1, keepdims=True),
            jnp.sum(jnp.where(lane == bcol, beta_all, 0.0), axis=1, keepdims=True),
            jnp.sum(jnp.where(sub == gcol, g_all_t, 0.0), axis=0, keepdims=True),
        )

    gc = {}
    for hh, cc in units:
        g_colv, _, g_rowv = heads[hh]
        g_c = jnp.broadcast_to(g_colv[rows_of(cc), :], (c, LANES))
        gcum = _dot(cum_col3, jnp.concatenate(_split3(g_c), axis=0))
        g_r = jnp.broadcast_to(g_rowv[:, cc * c:(cc + 1) * c], (8, c))
        gcum_r = _dot(jnp.concatenate(_split3(g_r), axis=1), cum_row3)[0:1, :]
        gc[hh, cc] = (gcum, gcum_r)

    st = {}
    for hh, cc in units:
        gcum, gcum_r = gc[hh, cc]
        q = q_ref[rows_of(cc), cols_of(hh)]
        k = k_ref[rows_of(cc), cols_of(hh)]
        decay = jnp.where(m_incl, jnp.exp(jnp.minimum(gcum[:, :c] - gcum_r, 0.0)), 0.0)
        beta = jnp.broadcast_to(heads[hh][1][rows_of(cc), :], (c, LANES))
        kb = k * beta
        both = _dot_nt(jnp.concatenate([kb, q], axis=0).astype(BF16), k.astype(BF16))
        n = -jnp.where(m_strict, both[:c] * decay, 0.0)
        st[hh, cc] = dict(attn=(both[c:] * decay).astype(BF16), n=n, x=eye + n, kb=kb, beta=beta)

    for u in units:
        st[u]["p"] = _dot_split(st[u]["n"], st[u]["n"])
    levels = int(math.log2(c)) - 1
    for lvl in range(levels):
        for u in units:
            s_u = st[u]
            if lvl == levels - 1:
                s_u["x"] = s_u["x"] + _dot_split(s_u["p"], s_u["x"])
            else:
                xp = _dot_split(s_u["p"], jnp.concatenate([s_u["x"], s_u["p"]], axis=1))
                s_u["x"] = s_u["x"] + xp[:, :c]
                s_u["p"] = xp[:, c:]

    local = {}
    for hh, cc in units:
        s_u = st[hh, cc]
        gcum, _ = gc[hh, cc]
        g_last = gcum[0:1, :] if rev else gcum[c - 1:c, :]
        q = q_ref[rows_of(cc), cols_of(hh)]
        k = k_ref[rows_of(cc), cols_of(hh)]
        v = v_ref[rows_of(cc), cols_of(hh)]
        e_g = jnp.exp(gcum)
        uw = _dot_split(s_u["x"], jnp.concatenate([v * s_u["beta"], s_u["kb"] * e_g], axis=1))
        wq = jnp.concatenate([uw[:, GDN_DV:], q * e_g], axis=0).astype(BF16)
        k_dec = (k * jnp.exp(g_last - gcum)).astype(BF16)
        local[hh, cc] = (uw[:, :GDN_DV], wq, s_u["attn"], k_dec, jnp.exp(g_last))

    states = [s_scr[hh] for hh in range(hp)]
    for cc in chunk_order:
        for hh in range(hp):
            rows, cols = rows_of(cc), cols_of(hh)
            s = states[hh]
            u, wq, attn, k_dec, e_last = local[hh, cc]
            ws = _dot(wq, s.astype(BF16))
            vb = (u - ws[:c]).astype(BF16)
            o = ws[c:] + _dot(attn, vb)
            states[hh] = s * e_last + _dot_tn(k_dec, vb)
            if finish:
                o = o + of_ref[rows, cols]
                z = z_ref[rows, cols]
                o = o * lax.rsqrt(jnp.mean(o * o, axis=-1, keepdims=True) + NORM_EPS) * ng_ref[...]
                o_ref[rows, cols] = (o * (z * _sigmoid(z))).astype(o_ref.dtype)
            else:
                o_ref[rows, cols] = o
    for hh in range(hp):
        s_scr[hh] = states[hh]

    if emit_state:
        @pl.when(tb == ntb - 1)
        def _():
            sfin_ref[...] = s_scr[...]


def _gdn_scan(qkv, tail, tail_t, gate_params, il, row0, nb, n, *, rev, s0=None, o_fwd=None, z_src=None,
              norm_g=None, emit_state=False, prev=None):
    tb_sz = GDN_TB
    ntb = n // tb_sz
    finish = o_fwd is not None
    d = 1 if rev else 0
    a_row, dt_row, a_col, dt_col = gate_params

    def rowblk(b, t):
        tt = (ntb - 1 - t) if rev else t
        return row0 // tb_sz + b * ntb + tt

    hp = GDN_HP
    hw = hp * LANES
    ngrp = GDN_HEADS // hp
    args = [qkv, qkv, qkv, tail, tail_t, a_row, dt_row, a_col, dt_col]
    in_specs = [
        pl.BlockSpec((tb_sz, hw), lambda b, h, t: (rowblk(b, t), h)),
        pl.BlockSpec((tb_sz, hw), lambda b, h, t: (rowblk(b, t), ngrp + h)),
        pl.BlockSpec((tb_sz, hw), lambda b, h, t: (rowblk(b, t), 2 * ngrp + h)),
        pl.BlockSpec((tb_sz, LANES), lambda b, h, t: (rowblk(b, t), 0)),
        pl.BlockSpec((LANES, tb_sz), lambda b, h, t: (0, rowblk(b, t))),
        pl.BlockSpec((None, 1, LANES), lambda b, h, t: (il, 0, 0)),
        pl.BlockSpec((None, 1, LANES), lambda b, h, t: (il, 0, 0)),
        pl.BlockSpec((None, LANES, 1), lambda b, h, t: (il, 0, 0)),
        pl.BlockSpec((None, LANES, 1), lambda b, h, t: (il, 0, 0)),
    ]
    if s0 is not None:
        args.append(s0)
        in_specs.append(pl.BlockSpec((None, None, None, hp, GDN_DK, GDN_DV),
                                     lambda b, h, t: (b, il, d, h, 0, 0)))
    if finish:
        zcol0 = (FNET_WIDTH + GDN_QKV) // hw
        args += [o_fwd, z_src, norm_g]
        in_specs += [
            pl.BlockSpec((tb_sz, hw), lambda b, h, t: (rowblk(b, t), h)),
            pl.BlockSpec((tb_sz, hw), lambda b, h, t: (rowblk(b, t), zcol0 + h)),
            pl.BlockSpec((None, 1, GDN_DV), lambda b, h, t: (il, 0, 0)),
        ]
    out_dtype = BF16 if finish else F32
    out_shape = [jax.ShapeDtypeStruct((N_TOK, GDN_WIDTH), out_dtype)]
    out_specs = [pl.BlockSpec((tb_sz, hw), lambda b, h, t: (rowblk(b, t), h))]
    if emit_state:
        out_shape.append(jax.ShapeDtypeStruct((nb, GDN_HEADS, GDN_DK, GDN_DV), F32))
        out_specs.append(pl.BlockSpec((None, hp, GDN_DK, GDN_DV), lambda b, h, t: (b, h, 0, 0)))
    kwargs = {}
    n_in = len(args)
    kern = functools.partial(_gdn_kernel, rev=rev, has_s0=s0 is not None, finish=finish,
                             emit_state=emit_state, ntb=ntb)
    if prev is not None:
        args.append(prev)
        in_specs.append(pl.BlockSpec(memory_space=pl.ANY))
        kwargs["input_output_aliases"] = {n_in: 0}
        kern = functools.partial(_gdn_alias_kernel, n_in=n_in, inner=kern)
    res = pl.pallas_call(
        kern,
        grid=(nb, ngrp, ntb),
        in_specs=in_specs,
        out_specs=out_specs,
        out_shape=out_shape,
        scratch_shapes=[pltpu.VMEM((hp, GDN_DK, GDN_DV), F32)],
        compiler_params=_cparams(("arbitrary", "arbitrary", "arbitrary")),
        name=f"gdn_{'bwd' if rev else 'fwd'}_{n}",
        **kwargs,
    )(*args)
    return res


def _gdn_alias_kernel(*refs, n_in, inner):
    inner(*(refs[:n_in] + refs[n_in + 1:]))


def _attn_kernel(*refs, lam_init, has_cache):
    if has_cache:
        q_ref, k_ref, v_ref, kc_ref, vc_ref, lam_ref, g_ref, o_ref = refs
    else:
        q_ref, k_ref, v_ref, lam_ref, g_ref, o_ref = refs
    lv = lam_ref[...]
    lam = (jnp.exp(jnp.sum(lv[0:1, :] * lv[1:2, :], axis=-1, keepdims=True))
           - jnp.exp(jnp.sum(lv[2:3, :] * lv[3:4, :], axis=-1, keepdims=True)) + lam_init)
    q = q_ref[...] * (DIFF_DH ** -0.5)
    lane = lax.broadcasted_iota(jnp.int32, q.shape, 1)
    qs = (jnp.where(lane < DIFF_DH, q, 0.0).astype(BF16), jnp.where(lane >= DIFF_DH, q, 0.0).astype(BF16))
    kb = k_ref[...].astype(BF16)
    kcb = kc_ref[...].astype(BF16) if has_cache else None
    probs = []
    for qh in qs:
        s = _dot_nt(qh, kb)
        m = jnp.max(s, axis=-1, keepdims=True)
        if has_cache:
            sc = _dot_nt(qh, kcb)
            m = jnp.maximum(m, jnp.max(sc, axis=-1, keepdims=True))
            ec = jnp.exp(sc - m)
        e = jnp.exp(s - m)
        den = jnp.sum(e, axis=-1, keepdims=True)
        if has_cache:
            den = den + jnp.sum(ec, axis=-1, keepdims=True)
        inv = 1.0 / den
        probs.append((e * inv, ec * inv if has_cache else None))
    a = probs[0][0] - lam * probs[1][0]
    o = _dot(a.astype(BF16), v_ref[...].astype(BF16))
    if has_cache:
        ac = probs[0][1] - lam * probs[1][1]
        o = o + _dot(ac.astype(BF16), vc_ref[...].astype(BF16))
    o = o * lax.rsqrt(jnp.mean(o * o, axis=-1, keepdims=True) + NORM_EPS) * g_ref[...] * (1.0 - lam_init)
    o_ref[...] = o.astype(o_ref.dtype)


def _attn_prompt(p_odd, lam_rows, norm_g, il, lam_init):
    n = SEQ
    return pl.pallas_call(
        functools.partial(_attn_kernel, lam_init=lam_init, has_cache=False),
        grid=(BATCH, DIFF_HEADS),
        in_specs=[
            pl.BlockSpec((n, LANES), lambda b, h: (b, h)),
            pl.BlockSpec((n, LANES), lambda b, h: (b, DIFF_HEADS + h)),
            pl.BlockSpec((n, LANES), lambda b, h: (b, 2 * DIFF_HEADS + h)),
            pl.BlockSpec((None, 8, LANES), lambda b, h: (il, 0, 0)),
            pl.BlockSpec((None, 1, DIFF_DV), lambda b, h: (il, 0, 0)),
        ],
        out_specs=pl.BlockSpec((n, LANES), lambda b, h: (b, h)),
        out_shape=jax.ShapeDtypeStruct((N_TOK, DIFF_WIDTH), BF16),
        compiler_params=_cparams(("arbitrary", "arbitrary")),
        name="attn_prompt",
    )(p_odd, p_odd, p_odd, lam_rows, norm_g)


def _attn_sample_kernel(*refs, lam_init):
    _attn_kernel(*(refs[:7] + refs[8:]), lam_init=lam_init, has_cache=True)


def _attn_sample(p_odd, cache_k, cache_v, lam_rows, norm_g, il, lam_init, o_prev):
    tq = TQ_ATT
    nq = DEC_SEQ // tq
    kblk = N_P // DEC_SEQ
    return pl.pallas_call(
        functools.partial(_attn_sample_kernel, lam_init=lam_init),
        grid=(DEC_BATCH, DIFF_HEADS, nq),
        in_specs=[
            pl.BlockSpec((tq, LANES), lambda b, h, i: (N_P // tq + b * nq + i, h)),
            pl.BlockSpec((DEC_SEQ, LANES), lambda b, h, i: (kblk + b, DIFF_HEADS + h)),
            pl.BlockSpec((DEC_SEQ, LANES), lambda b, h, i: (kblk + b, 2 * DIFF_HEADS + h)),
            pl.BlockSpec((None, None, PAST_LEN, LANES), lambda b, h, i: (b, il, 0, h)),
            pl.BlockSpec((None, None, PAST_LEN, LANES), lambda b, h, i: (b, il, 0, h)),
            pl.BlockSpec((None, 8, LANES), lambda b, h, i: (il, 0, 0)),
            pl.BlockSpec((None, 1, DIFF_DV), lambda b, h, i: (il, 0, 0)),
            pl.BlockSpec(memory_space=pl.ANY),
        ],
        out_specs=pl.BlockSpec((tq, LANES), lambda b, h, i: (N_P // tq + b * nq + i, h)),
        out_shape=jax.ShapeDtypeStruct((N_TOK, DIFF_WIDTH), BF16),
        input_output_aliases={7: 0},
        compiler_params=_cparams(("arbitrary", "arbitrary", "arbitrary")),
        name="attn_sample",
    )(p_odd, p_odd, p_odd, cache_k, cache_v, lam_rows, norm_g, o_prev)


def _outproj_kernel(*refs, n_parts, ln_row):
    parts = refs[:n_parts]
    ws = refs[n_parts:2 * n_parts]
    x_ref, mod_ref, lng_ref, lnb_ref, o_ref = refs[2 * n_parts:]
    y = _dot(parts[0][...], ws[0][...])
    for p_ref, w_ref in zip(parts[1:], ws[1:]):
        y = y + _dot(p_ref[...], w_ref[...])
    r = ALPHA * x_ref[...] + mod_ref[2:3, :] * y
    o_ref[...] = _layer_norm(r, lng_ref[ln_row:ln_row + 1, :], lnb_ref[ln_row:ln_row + 1, :])


def _outproj_ln(parts, w_all, il, x, mod_all, layer, ln_g, ln_b):
    tm = TM_LN
    n_parts = len(parts)
    in_specs = []
    for p in parts:
        in_specs.append(pl.BlockSpec((tm, p.shape[1]), lambda i: (i, 0)))
    off = 0
    for idx, p in enumerate(parts):
        wdt = p.shape[1]
        in_specs.append(pl.BlockSpec((None, wdt, D_MODEL), functools.partial(lambda i, o: (il, o, 0), o=off // wdt)))
        off += wdt
    in_specs += [
        pl.BlockSpec((tm, D_MODEL), lambda i: (i, 0)),
        pl.BlockSpec((None, None, 8, D_MODEL), lambda i: (layer, _group_of_row(i * tm), 0, 0)),
        pl.BlockSpec((2 * DEPTH, D_MODEL), lambda i: (0, 0)),
        pl.BlockSpec((2 * DEPTH, D_MODEL), lambda i: (0, 0)),
    ]
    return pl.pallas_call(
        functools.partial(_outproj_kernel, n_parts=n_parts, ln_row=2 * layer),
        grid=(N_TOK // tm,),
        in_specs=in_specs,
        out_specs=pl.BlockSpec((tm, D_MODEL), lambda i: (i, 0)),
        out_shape=jax.ShapeDtypeStruct((N_TOK, D_MODEL), F32),
        compiler_params=_cparams(("arbitrary",)),
        name="outproj_ln",
    )(*parts, *([w_all] * n_parts), x, mod_all, ln_g, ln_b)


def _ffn_dense_kernel(x_ref, mod_ref, lng_ref, lnb_ref, wg_ref, wu_ref, wd_ref, o_ref, h_scr, acc_scr, *,
                      ln_row, nf):
    j = pl.program_id(1)

    @pl.when(j == 0)
    def _():
        h_scr[...] = _modulated(x_ref, mod_ref, 3).astype(BF16)
        acc_scr[...] = jnp.zeros_like(acc_scr)

    hb = h_scr[...]
    g = _dot(hb, wg_ref[...])
    u = _dot(hb, wu_ref[...])
    a = (g * _sigmoid(g) * u).astype(BF16)
    acc_scr[...] += _dot(a, wd_ref[...])

    @pl.when(j == nf - 1)
    def _():
        r = ALPHA * x_ref[...] + mod_ref[5:6, :] * acc_scr[...]
        o_ref[...] = _layer_norm(r, lng_ref[ln_row:ln_row + 1, :], lnb_ref[ln_row:ln_row + 1, :])


def _ffn_dense(x, mod_all, layer, ln_g, ln_b, w_gu, w_down, il):
    tm, tf = TM_FFN, TF_FFN
    nf = FFN_DIM // tf
    return pl.pallas_call(
        functools.partial(_ffn_dense_kernel, ln_row=2 * layer + 1, nf=nf),
        grid=(N_TOK // tm, nf),
        in_specs=[
            pl.BlockSpec((tm, D_MODEL), lambda i, j: (i, 0)),
            pl.BlockSpec((None, None, 8, D_MODEL), lambda i, j: (layer, _group_of_row(i * tm), 0, 0)),
            pl.BlockSpec((2 * DEPTH, D_MODEL), lambda i, j: (0, 0)),
            pl.BlockSpec((2 * DEPTH, D_MODEL), lambda i, j: (0, 0)),
            pl.BlockSpec((None, D_MODEL, tf), lambda i, j: (il, 0, j)),
            pl.BlockSpec((None, D_MODEL, tf), lambda i, j: (il, 0, nf + j)),
            pl.BlockSpec((None, tf, D_MODEL), lambda i, j: (il, j, 0)),
        ],
        out_specs=pl.BlockSpec((tm, D_MODEL), lambda i, j: (i, 0)),
        out_shape=jax.ShapeDtypeStruct((N_TOK, D_MODEL), F32),
        scratch_shapes=[pltpu.VMEM((tm, D_MODEL), BF16), pltpu.VMEM((tm, D_MODEL), F32)],
        compiler_params=_cparams(("arbitrary", "arbitrary")),
        name="ffn_dense",
    )(x, mod_all, ln_g, ln_b, w_gu, w_gu, w_down)


def _router_kernel(x_ref, mod_ref, wr_ref, h_ref, idx_ref, gate_ref):
    h = _modulated(x_ref, mod_ref, 3)
    h_ref[...] = h
    logits = _dot(h, wr_ref[...], HI)
    lane_i = lax.broadcasted_iota(jnp.int32, logits.shape, 1)
    lane = lane_i.astype(F32)
    neg = -1e30
    logits = jnp.where(lane_i < N_EXPERTS, logits, neg)
    m1 = jnp.max(logits, axis=-1, keepdims=True)
    i1 = jnp.min(jnp.where(logits == m1, lane, float(LANES)), axis=-1, keepdims=True)
    rest = jnp.where(lane == i1, neg, logits)
    m2 = jnp.max(rest, axis=-1, keepdims=True)
    i2 = jnp.min(jnp.where(rest == m2, lane, float(LANES)), axis=-1, keepdims=True)
    e = jnp.exp(m2 - m1)
    g1 = 1.0 / (1.0 + e)
    g2 = e / (1.0 + e)
    idx_ref[...] = jnp.where(lane_i == 0, i1, jnp.where(lane_i == 1, i2, 0.0)).astype(jnp.int32)
    gate_ref[...] = jnp.where(lane_i == 0, g1, jnp.where(lane_i == 1, g2, 0.0))


def _router(x, mod_all, layer, router_w_pad, il):
    tm = TM_TOK
    return pl.pallas_call(
        _router_kernel,
        grid=(N_TOK // tm,),
        in_specs=[
            pl.BlockSpec((tm, D_MODEL), lambda i: (i, 0)),
            pl.BlockSpec((None, None, 8, D_MODEL), lambda i: (layer, _group_of_row(i * tm), 0, 0)),
            pl.BlockSpec((None, D_MODEL, LANES), lambda i: (il, 0, 0)),
        ],
        out_specs=[
            pl.BlockSpec((tm, D_MODEL), lambda i: (i, 0)),
            pl.BlockSpec((tm, LANES), lambda i: (i, 0)),
            pl.BlockSpec((tm, LANES), lambda i: (i, 0)),
        ],
        out_shape=[
            jax.ShapeDtypeStruct((N_TOK, D_MODEL), F32),
            jax.ShapeDtypeStruct((N_TOK, LANES), jnp.int32),
            jax.ShapeDtypeStruct((N_TOK, LANES), F32),
        ],
        compiler_params=_cparams(("arbitrary",)),
        name="router",
    )(x, mod_all, router_w_pad)


def _routing_plan(idx):
    tm = TM_MOE
    e_flat = idx[:, :TOP_K].reshape(-1)
    onehot = (e_flat[:, None] == jnp.arange(N_EXPERTS, dtype=jnp.int32)[None, :]).astype(jnp.int32)
    csum = jnp.cumsum(onehot, axis=0)
    rank = jnp.take_along_axis(csum, e_flat[:, None], axis=1)[:, 0] - 1
    counts = csum[-1]
    tiles_per = (counts + tm - 1) // tm
    tile_end = jnp.cumsum(tiles_per)
    tile_start = tile_end - tiles_per
    pos = tile_start[e_flat] * tm + rank
    n_used = tile_end[-1]
    tile_ids = jnp.arange(MOE_TILES, dtype=jnp.int32)
    tile_expert = jnp.sum((tile_ids[:, None] >= tile_end[None, :]).astype(jnp.int32), axis=1)
    last_expert = jnp.max(jnp.where(counts > 0, jnp.arange(N_EXPERTS, dtype=jnp.int32), 0))
    tile_expert = jnp.minimum(tile_expert, last_expert).astype(jnp.int32)
    tok = jnp.arange(TOP_K * N_TOK, dtype=jnp.int32) // TOP_K
    row_tok = jnp.zeros((MOE_ROWS,), jnp.int32).at[pos].set(tok, unique_indices=True)
    return pos.astype(jnp.int32), row_tok, tile_expert, n_used.astype(jnp.int32).reshape(1)


def _ffn_moe_kernel(texp_ref, nused_ref, rowtok_ref, h_hbm, wg_ref, wu_ref, wd_ref, o_ref,
                    h_scr, acc_scr, sem, *, tm, nf):
    i = pl.program_id(0)
    j = pl.program_id(1)
    used = i < nused_ref[0]

    def row_copy(r):
        return pltpu.make_async_copy(h_hbm.at[pl.ds(rowtok_ref[i * tm + r], 1), :],
                                     acc_scr.at[pl.ds(r, 1), :], sem)

    @pl.when(jnp.logical_and(used, j == 0))
    def _():
        def start(r, carry):
            row_copy(r).start()
            return carry

        lax.fori_loop(0, tm, start, 0)

        def wait(r, carry):
            row_copy(r).wait()
            return carry

        lax.fori_loop(0, tm, wait, 0)
        h_scr[...] = acc_scr[...].astype(BF16)
        acc_scr[...] = jnp.zeros_like(acc_scr)

    @pl.when(used)
    def _():
        hb = h_scr[...]
        g = _dot(hb, wg_ref[...].astype(BF16))
        u = _dot(hb, wu_ref[...].astype(BF16))
        a = (g * _sigmoid(g) * u).astype(BF16)
        acc_scr[...] += _dot(a, wd_ref[...].astype(BF16))

    @pl.when(jnp.logical_and(used, j == nf - 1))
    def _():
        o_ref[...] = acc_scr[...]

    @pl.when(jnp.logical_and(jnp.logical_not(used), j == nf - 1))
    def _():
        o_ref[...] = jnp.zeros_like(o_ref)


def _ffn_moe(h, plan, w_gu, w_down, il):
    _, row_tok, tile_expert, n_used = plan
    tm, tf = TM_MOE, TF_MOE
    nf = EXPERT_DIM // tf

    def wmap(off):
        def f(i, j, texp, nused, rowtok):
            live = i < nused[0]
            ii = jnp.minimum(i, nused[0] - 1)
            return (il, texp[ii], 0, off + jnp.where(live, j, nf - 1))
        return f

    def dmap(i, j, texp, nused, rowtok):
        live = i < nused[0]
        ii = jnp.minimum(i, nused[0] - 1)
        return (il, texp[ii], jnp.where(live, j, nf - 1), 0)

    grid_spec = pltpu.PrefetchScalarGridSpec(
        num_scalar_prefetch=3,
        grid=(MOE_TILES, nf),
        in_specs=[
            pl.BlockSpec(memory_space=pl.ANY),
            pl.BlockSpec((None, None, D_MODEL, tf), wmap(0)),
            pl.BlockSpec((None, None, D_MODEL, tf), wmap(nf)),
            pl.BlockSpec((None, None, tf, D_MODEL), dmap),
        ],
        out_specs=pl.BlockSpec((tm, D_MODEL), lambda i, j, *_: (i, 0), pipeline_mode=pl.Buffered(1)),
        scratch_shapes=[
            pltpu.VMEM((tm, D_MODEL), BF16),
            pltpu.VMEM((tm, D_MODEL), F32),
            pltpu.SemaphoreType.DMA(()),
        ],
    )
    return pl.pallas_call(
        functools.partial(_ffn_moe_kernel, tm=tm, nf=nf),
        grid_spec=grid_spec,
        out_shape=jax.ShapeDtypeStruct((MOE_ROWS, D_MODEL), F32),
        compiler_params=_cparams(("arbitrary", "arbitrary")),
        name="ffn_moe",
    )(tile_expert, n_used, row_tok, h, w_gu, w_gu, w_down)


def _combine_kernel(pos_ref, y_hbm, gate_ref, x_ref, mod_ref, lng_ref, lnb_ref, o_ref, ybuf, sem, *, tm, ln_row):
    i = pl.program_id(0)

    def row_copy(r, k):
        return pltpu.make_async_copy(y_hbm.at[pl.ds(pos_ref[TOP_K * (i * tm + r) + k], 1), :],
                                     ybuf.at[k, pl.ds(r, 1), :], sem)

    def start(r, carry):
        for k in range(TOP_K):
            row_copy(r, k).start()
        return carry

    lax.fori_loop(0, tm, start, 0)

    def wait(r, carry):
        for k in range(TOP_K):
            row_copy(r, k).wait()
        return carry

    lax.fori_loop(0, tm, wait, 0)
    gates = gate_ref[...]
    f = gates[:, 0:1] * ybuf[0] + gates[:, 1:2] * ybuf[1]
    r = ALPHA * x_ref[...] + mod_ref[5:6, :] * f
    o_ref[...] = _layer_norm(r, lng_ref[ln_row:ln_row + 1, :], lnb_ref[ln_row:ln_row + 1, :])


def _moe_combine(y_rows, pos, gates, x, mod_all, layer, ln_g, ln_b):
    tm = TM_TOK
    grid_spec = pltpu.PrefetchScalarGridSpec(
        num_scalar_prefetch=1,
        grid=(N_TOK // tm,),
        in_specs=[
            pl.BlockSpec(memory_space=pl.ANY),
            pl.BlockSpec((tm, LANES), lambda i, *_: (i, 0)),
            pl.BlockSpec((tm, D_MODEL), lambda i, *_: (i, 0)),
            pl.BlockSpec((None, None, 8, D_MODEL), lambda i, *_: (layer, _group_of_row(i * tm), 0, 0)),
            pl.BlockSpec((2 * DEPTH, D_MODEL), lambda i, *_: (0, 0)),
            pl.BlockSpec((2 * DEPTH, D_MODEL), lambda i, *_: (0, 0)),
        ],
        out_specs=pl.BlockSpec((tm, D_MODEL), lambda i, *_: (i, 0)),
        scratch_shapes=[pltpu.VMEM((TOP_K, tm, D_MODEL), F32), pltpu.SemaphoreType.DMA(())],
    )
    return pl.pallas_call(
        functools.partial(_combine_kernel, tm=tm, ln_row=2 * layer + 1),
        grid_spec=grid_spec,
        out_shape=jax.ShapeDtypeStruct((N_TOK, D_MODEL), F32),
        compiler_params=_cparams(("arbitrary",)),
        name="moe_combine",
    )(pos, y_rows, gates, x, mod_all, ln_g, ln_b)


def _pad_lanes(a, lane0=0):
    width = a.shape[-1]
    pads = [(0, 0)] * (a.ndim - 1) + [(lane0, LANES - lane0 - width)]
    return jnp.pad(a, pads)


def kernel(x_prompt, x_sample, state_gdn, cache_diff_k, cache_diff_v, c, c_ctx, w_ada, b_ada, ln_g, ln_b,
           w_in_even, gdn_conv_w, gdn_a_log, gdn_dt_bias, gdn_norm_g, w_out_even, w_in_odd,
           lam_q1, lam_k1, lam_q2, lam_k2, diff_norm_g, w_out_odd, ffn_w_gu, ffn_w_down,
           router_w, moe_w_gu, moe_w_down):
    x = jnp.concatenate([x_prompt.reshape(N_P, D_MODEL), x_sample.reshape(N_S, D_MODEL)], axis=0)

    cond8 = jnp.concatenate([c_ctx[None, :], c, jnp.zeros((8 - N_GROUPS, D_MODEL), F32)], axis=0)
    mod = _ada_modulation(cond8, w_ada, b_ada)
    mod_all = jnp.pad(mod.reshape(DEPTH, 8, 6, D_MODEL)[:, :N_GROUPS], ((0, 0), (0, 0), (0, 2), (0, 0)))

    ln_g2 = ln_g.reshape(2 * DEPTH, D_MODEL)
    ln_b2 = ln_b.reshape(2 * DEPTH, D_MODEL)

    w_even_main = w_in_even[:, :, :EVEN_MAIN].astype(BF16)
    w_even_tail = _pad_lanes(w_in_even[:, :, EVEN_MAIN:]).astype(BF16)
    w_even_tail_t = jnp.swapaxes(w_even_tail, 1, 2)
    w_out_even_b = w_out_even.astype(BF16)
    w_in_odd_b = w_in_odd.astype(BF16)
    w_out_odd_b = w_out_odd.astype(BF16)
    ffn_w_gu_b = ffn_w_gu.astype(BF16)
    ffn_w_down_b = ffn_w_down.astype(BF16)
    router_w_pad = _pad_lanes(router_w)

    a_row = _pad_lanes(gdn_a_log.reshape(N_EVEN, 1, 2 * GDN_HEADS), 2 * GDN_HEADS)
    dt_row = _pad_lanes(gdn_dt_bias.reshape(N_EVEN, 1, 2 * GDN_HEADS), 2 * GDN_HEADS)
    gate_params = (a_row, dt_row, jnp.swapaxes(a_row, 1, 2), jnp.swapaxes(dt_row, 1, 2))
    gdn_norm3 = gdn_norm_g.reshape(N_EVEN, 1, GDN_DV)

    rope = _rope_tables()
    lam_rows = jnp.pad(jnp.stack([lam_q1, lam_k1, lam_q2, lam_k2], axis=1),
                       ((0, 0), (0, 4), (0, LANES - DIFF_DH)))
    diff_norm3 = diff_norm_g.reshape(N_ODD, 1, DIFF_DV)
    cache_k = cache_diff_k.reshape(DEC_BATCH, N_ODD, PAST_LEN, DIFF_WIDTH)
    cache_v = cache_diff_v.reshape(DEC_BATCH, N_ODD, PAST_LEN, DIFF_WIDTH)

    gdn_states, ctx_keys, ctx_vals = [], [], []
    for layer in range(DEPTH):
        il = layer // 2
        if layer % 2 == 0:
            p_main, tail, tail_t = _inproj_even(x, mod_all, layer, w_even_main, w_even_tail, w_even_tail_t, il)
            y_a = _fnet(p_main, 0, BATCH, SEQ)
            y_a = _fnet(p_main, N_P, DEC_BATCH, DEC_SEQ, y_prev=y_a)
            qkv = _gdn_prep(p_main, gdn_conv_w, il, 0, BATCH, SEQ)
            qkv = _gdn_prep(p_main, gdn_conv_w, il, N_P, DEC_BATCH, DEC_SEQ, prev=qkv)
            common = (qkv, tail, tail_t, gate_params, il)
            o_f, s_f = _gdn_scan(*common, 0, BATCH, SEQ, rev=False, emit_state=True)
            (o_f,) = _gdn_scan(*common, N_P, DEC_BATCH, DEC_SEQ, rev=False, s0=state_gdn, prev=o_f)
            y_b, s_b = _gdn_scan(*common, 0, BATCH, SEQ, rev=True, o_fwd=o_f, z_src=p_main,
                                 norm_g=gdn_norm3, emit_state=True)
            (y_b,) = _gdn_scan(*common, N_P, DEC_BATCH, DEC_SEQ, rev=True, s0=state_gdn, o_fwd=o_f,
                               z_src=p_main, norm_g=gdn_norm3, prev=y_b)
            gdn_states.append(jnp.stack([s_f, s_b], axis=1))
            x = _outproj_ln([y_a, y_b], w_out_even_b, il, x, mod_all, layer, ln_g2, ln_b2)
            x = _ffn_dense(x, mod_all, layer, ln_g2, ln_b2, ffn_w_gu_b, ffn_w_down_b, il)
        else:
            lam_init = 0.8 - 0.6 * math.exp(-0.3 * layer)
            p_odd = _inproj_odd(x, mod_all, layer, w_in_odd_b, il, rope)
            ctx_keys.append(p_odd[:N_P, DIFF_WIDTH:2 * DIFF_WIDTH].reshape(BATCH, SEQ, 2 * DIFF_HEADS, DIFF_DH))
            ctx_vals.append(p_odd[:N_P, 2 * DIFF_WIDTH:].reshape(BATCH, SEQ, DIFF_HEADS, DIFF_DV))
            o = _attn_prompt(p_odd, lam_rows, diff_norm3, il, lam_init)
            o = _attn_sample(p_odd, cache_k, cache_v, lam_rows, diff_norm3, il, lam_init, o)
            x = _outproj_ln([o], w_out_odd_b, il, x, mod_all, layer, ln_g2, ln_b2)
            h, idx, gates = _router(x, mod_all, layer, router_w_pad, il)
            plan = _routing_plan(idx)
            y_rows = _ffn_moe(h, plan, moe_w_gu, moe_w_down, il)
            x = _moe_combine(y_rows, plan[0], gates, x, mod_all, layer, ln_g2, ln_b2)

    y_prompt = x[:N_P].reshape(BATCH, SEQ, D_MODEL)
    y_sample = x[N_P:].reshape(DEC_BATCH, DEC_SEQ, D_MODEL)
    new_state_gdn = jnp.stack(gdn_states, axis=1)
    new_cache_k = jnp.stack(ctx_keys, axis=1)
    new_cache_v = jnp.stack(ctx_vals, axis=1)
    return (y_prompt, y_sample, new_state_gdn, new_cache_k, new_cache_v)
```

```python
import functools
import math

import numpy as np
import jax
import jax.numpy as jnp
from jax import lax
from jax.experimental import pallas as pl
from jax.experimental.pallas import tpu as pltpu

F32 = jnp.float32
BF16 = jnp.bfloat16
HI = lax.Precision.HIGHEST

D_MODEL = 2048
BATCH = 16
SEQ = 256
DEPTH = 4
DEC_BATCH = 4
DEC_SEQ = 2048
PAST_LEN = 256
GRID_W = 64
N_EVEN = (DEPTH + 1) // 2
N_ODD = DEPTH // 2
FNET_GROUPS = 4
FNET_GROUP_W = D_MODEL // 8
FNET_WIDTH = FNET_GROUPS * FNET_GROUP_W
GDN_DK = 128
GDN_DV = 128
GDN_HEADS = (D_MODEL // 2) // GDN_DV
GDN_WIDTH = GDN_HEADS * GDN_DV
GDN_QKV = GDN_HEADS * (2 * GDN_DK + GDN_DV)
GDN_CONV = 3
EVEN_MAIN = FNET_WIDTH + GDN_QKV + GDN_WIDTH
DIFF_DH = 64
DIFF_DV = 2 * DIFF_DH
DIFF_HEADS = D_MODEL // DIFF_DV
DIFF_WIDTH = DIFF_HEADS * DIFF_DV
ROPE_BASE = 10000.0
FFN_DIM = 5632
N_EXPERTS = 8
TOP_K = 2
EXPERT_DIM = 7168
ALPHA = (2 * DEPTH) ** 0.25
LN_EPS = 1e-5
NORM_EPS = 1e-6

N_P = BATCH * SEQ
N_S = DEC_BATCH * DEC_SEQ
N_TOK = N_P + N_S
N_GROUPS = 1 + DEC_BATCH

LANES = 128
VMEM_LIMIT = 56 * 1024 * 1024
GDN_CHUNK = 64
GDN_TB = 256
GDN_HP = 4
TM_PROJ = 1024
TN_PROJ = 1024
TM_LN = 512
TM_FFN = 512
TF_FFN = 512
TM_MOE = 640
TF_MOE = 512
TM_TOK = 256
TQ_ATT = 256
TI_FNET = 512
MOE_TILES = -(-(TOP_K * N_TOK + N_EXPERTS * (TM_MOE - 1)) // TM_MOE)
MOE_ROWS = MOE_TILES * TM_MOE


def _cparams(sem):
    return pltpu.CompilerParams(dimension_semantics=sem, vmem_limit_bytes=VMEM_LIMIT)


def _group_of_row(r0):
    return jnp.where(r0 < N_P, 0, 1 + (r0 - N_P) // DEC_SEQ)


def _sigmoid(x):
    return 1.0 / (1.0 + jnp.exp(-x))


def _softplus(x):
    return jnp.maximum(x, 0.0) + jnp.log1p(jnp.exp(-jnp.abs(x)))


def _dot(a, b, precision=None):
    return jnp.dot(a, b, preferred_element_type=F32, precision=precision)


def _dot_nt(a, b):
    return lax.dot_general(a, b, (((1,), (1,)), ((), ())), preferred_element_type=F32)


def _dot_tn(a, b):
    return lax.dot_general(a, b, (((0,), (0,)), ((), ())), preferred_element_type=F32)


def _layer_norm(r, g, b):
    mu = jnp.mean(r, axis=-1, keepdims=True)
    d = r - mu
    var = jnp.mean(d * d, axis=-1, keepdims=True)
    return d * lax.rsqrt(var + LN_EPS) * g + b


def _ada_kernel(cond_ref, w_ref, b_ref, o_ref):
    c = cond_ref[...]
    s = c * _sigmoid(c)
    o_ref[...] = _dot(s, w_ref[...], HI) + b_ref[...]


def _ada_modulation(cond8, w_ada, b_ada):
    tn = 1024
    return pl.pallas_call(
        _ada_kernel,
        grid=(DEPTH, 6 * D_MODEL // tn),
        in_specs=[
            pl.BlockSpec((8, D_MODEL), lambda l, j: (0, 0)),
            pl.BlockSpec((None, D_MODEL, tn), lambda l, j: (l, 0, j)),
            pl.BlockSpec((None, 1, tn), lambda l, j: (l, 0, j)),
        ],
        out_specs=pl.BlockSpec((None, 8, tn), lambda l, j: (l, 0, j)),
        out_shape=jax.ShapeDtypeStruct((DEPTH, 8, 6 * D_MODEL), F32),
        compiler_params=_cparams(("arbitrary", "arbitrary")),
        name="ada_modulation",
    )(cond8, w_ada, b_ada.reshape(DEPTH, 1, 6 * D_MODEL))


def _modulated(x_ref, mod_ref, shift_row):
    m = mod_ref[...]
    return x_ref[...] * (1.0 + m[shift_row + 1:shift_row + 2, :]) + m[shift_row:shift_row + 1, :]


def _inproj_even_kernel(x_ref, mod_ref, w_ref, wt_ref, wtt_ref, o_ref, ot_ref, ott_ref, h_scr):
    @pl.when(pl.program_id(1) == 0)
    def _():
        hb = _modulated(x_ref, mod_ref, 0).astype(BF16)
        h_scr[...] = hb
        ot_ref[...] = _dot(hb, wt_ref[...])
        ott_ref[...] = _dot_nt(wtt_ref[...], hb)

    o_ref[...] = _dot(h_scr[...], w_ref[...])


def _inproj_even(x, mod_all, layer, w_main, w_tail, w_tail_t, il):
    tm, tn = TM_PROJ, TN_PROJ
    return pl.pallas_call(
        _inproj_even_kernel,
        grid=(N_TOK // tm, EVEN_MAIN // tn),
        in_specs=[
            pl.BlockSpec((tm, D_MODEL), lambda i, j: (i, 0)),
            pl.BlockSpec((None, None, 8, D_MODEL), lambda i, j: (layer, _group_of_row(i * tm), 0, 0)),
            pl.BlockSpec((None, D_MODEL, tn), lambda i, j: (il, 0, j)),
            pl.BlockSpec((None, D_MODEL, LANES), lambda i, j: (il, 0, 0)),
            pl.BlockSpec((None, LANES, D_MODEL), lambda i, j: (il, 0, 0)),
        ],
        out_specs=[
            pl.BlockSpec((tm, tn), lambda i, j: (i, j)),
            pl.BlockSpec((tm, LANES), lambda i, j: (i, 0)),
            pl.BlockSpec((LANES, tm), lambda i, j: (0, i)),
        ],
        out_shape=[
            jax.ShapeDtypeStruct((N_TOK, EVEN_MAIN), F32),
            jax.ShapeDtypeStruct((N_TOK, LANES), F32),
            jax.ShapeDtypeStruct((LANES, N_TOK), F32),
        ],
        scratch_shapes=[pltpu.VMEM((tm, D_MODEL), BF16)],
        compiler_params=_cparams(("arbitrary", "arbitrary")),
        name="inproj_even",
    )(x, mod_all, w_main, w_tail, w_tail_t)


def _inproj_odd_kernel(x_ref, mod_ref, w_ref, cos_ref, sa_ref, sb_ref, o_ref, h_scr, *, tm, tn):
    i = pl.program_id(0)
    j = pl.program_id(1)

    @pl.when(j == 0)
    def _():
        h_scr[...] = _modulated(x_ref, mod_ref, 0).astype(BF16)

    y = _dot(h_scr[...], w_ref[...])
    is_rope = jnp.logical_and(i * tm >= N_P, j * tn < 2 * DIFF_WIDTH)

    @pl.when(is_rope)
    def _():
        c = cos_ref[...]
        sa = sa_ref[...]
        sb = sb_ref[...]
        for cb in range(tn // LANES):
            yc = y[:, cb * LANES:(cb + 1) * LANES]
            o_ref[:, cb * LANES:(cb + 1) * LANES] = (
                yc * c + pltpu.roll(yc, 16, 1) * sa + pltpu.roll(yc, LANES - 16, 1) * sb)

    @pl.when(jnp.logical_not(is_rope))
    def _():
        o_ref[...] = y


def _inproj_odd(x, mod_all, layer, w_in, il, rope):
    tm, tn = TM_PROJ, TN_PROJ

    def rope_map(i, j):
        r0 = i * tm
        return (jnp.where(r0 >= N_P, ((r0 - N_P) % DEC_SEQ) // tm, 0), 0)

    return pl.pallas_call(
        functools.partial(_inproj_odd_kernel, tm=tm, tn=tn),
        grid=(N_TOK // tm, 3 * DIFF_WIDTH // tn),
        in_specs=[
            pl.BlockSpec((tm, D_MODEL), lambda i, j: (i, 0)),
            pl.BlockSpec((None, None, 8, D_MODEL), lambda i, j: (layer, _group_of_row(i * tm), 0, 0)),
            pl.BlockSpec((None, D_MODEL, tn), lambda i, j: (il, 0, j)),
            pl.BlockSpec((tm, LANES), rope_map),
            pl.BlockSpec((tm, LANES), rope_map),
            pl.BlockSpec((tm, LANES), rope_map),
        ],
        out_specs=pl.BlockSpec((tm, tn), lambda i, j: (i, j)),
        out_shape=jax.ShapeDtypeStruct((N_TOK, 3 * DIFF_WIDTH), F32),
        scratch_shapes=[pltpu.VMEM((tm, D_MODEL), BF16)],
        compiler_params=_cparams(("arbitrary", "arbitrary")),
        name="inproj_odd",
    )(x, mod_all, w_in, *rope)


def _rope_tables():
    t = np.arange(DEC_SEQ)
    pos = np.stack([t // GRID_W, t % GRID_W], axis=1).astype(np.float32)
    axis_dim = DIFF_DH // 2
    half = axis_dim // 2
    inv_freq = (1.0 / (ROPE_BASE ** (np.arange(0, axis_dim, 2, dtype=np.float32) / axis_dim))).astype(np.float32)
    lane = np.arange(LANES)
    axis = (lane % DIFF_DH) // axis_dim
    within = lane % axis_dim
    ang = (pos[:, axis] * inv_freq[within % half][None, :]).astype(np.float32)
    cos = np.cos(ang).astype(np.float32)
    sin = np.sin(ang).astype(np.float32)
    second = (within >= half)[None, :]
    sa = np.where(second, sin, 0.0).astype(np.float32)
    sb = np.where(second, 0.0, -sin).astype(np.float32)
    return jnp.asarray(cos), jnp.asarray(sa), jnp.asarray(sb)


def _dft_tables(n):
    def cs(m):
        jk = (np.arange(m)[:, None] * np.arange(m)[None, :]) % m
        ang = 2.0 * np.pi * jk.astype(np.float64) / m
        return np.cos(ang), np.sin(ang)

    cc, sc = cs(FNET_GROUP_W)
    cn, sn = cs(n)
    chan = np.concatenate([cc, sc], axis=1)
    posn = np.concatenate([cn, -sn], axis=1)
    return jnp.asarray(chan, dtype=BF16), jnp.asarray(posn, dtype=BF16)


def _fnet_kernel(a_ref, chan_ref, posn_ref, o_ref, t_scr, *, n):
    @pl.when(pl.program_id(1) == 0)
    def _():
        a = a_ref[...].astype(BF16)
        w = FNET_GROUP_W
        for g in range(FNET_GROUPS):
            t = _dot(a[:, g * w:(g + 1) * w], chan_ref[...])
            t_scr[0:n, g * w:(g + 1) * w] = t[:, :w].astype(BF16)
            t_scr[n:2 * n, g * w:(g + 1) * w] = t[:, w:].astype(BF16)

    y = _dot(posn_ref[...], t_scr[...])
    o_ref[...] = (y * ((n * FNET_GROUP_W) ** -0.5)).astype(BF16)


def _fnet(p_main, row0, nb, n, y_prev=None):
    ti = min(TI_FNET, n)
    nt = n // ti
    chan, posn = _dft_tables(n)
    out_shape = jax.ShapeDtypeStruct((N_TOK, FNET_WIDTH), BF16)
    args = [p_main, chan, posn]
    in_specs = [
        pl.BlockSpec((n, FNET_WIDTH), lambda b, i: (row0 // n + b, 0)),
        pl.BlockSpec((FNET_GROUP_W, 2 * FNET_GROUP_W), lambda b, i: (0, 0)),
        pl.BlockSpec((ti, 2 * n), lambda b, i: (i, 0)),
    ]
    kwargs = {}
    kern = functools.partial(_fnet_kernel, n=n)
    if y_prev is not None:
        args.append(y_prev)
        in_specs.append(pl.BlockSpec(memory_space=pl.ANY))
        kwargs["input_output_aliases"] = {3: 0}
        kern = functools.partial(_fnet_alias_kernel, n=n)
    return pl.pallas_call(
        kern,
        grid=(nb, nt),
        in_specs=in_specs,
        out_specs=pl.BlockSpec((ti, FNET_WIDTH), lambda b, i: (row0 // ti + b * nt + i, 0)),
        out_shape=out_shape,
        scratch_shapes=[pltpu.VMEM((2 * n, FNET_WIDTH), BF16)],
        compiler_params=_cparams(("arbitrary", "arbitrary")),
        name=f"fnet_{n}",
        **kwargs,
    )(*args)


def _fnet_alias_kernel(a_ref, chan_ref, posn_ref, prev_ref, o_ref, t_scr, *, n):
    del prev_ref
    _fnet_kernel(a_ref, chan_ref, posn_ref, o_ref, t_scr, n=n)


def _gdn_prep_kernel(x_ref, w_ref, o_ref):
    j = pl.program_id(1)
    x = x_ref[...]
    n = x.shape[0]
    row = lax.broadcasted_iota(jnp.int32, x.shape, 0)
    x_prev = jnp.where(row == 0, 0.0, pltpu.roll(x, 1, 0))
    x_next = jnp.where(row == n - 1, 0.0, pltpu.roll(x, n - 1, 0))
    w = w_ref[...]
    y = x_prev * w[0:1, :] + x * w[1:2, :] + x_next * w[2:3, :]
    y = y * _sigmoid(y)
    inv = lax.rsqrt(jnp.sum(y * y, axis=-1, keepdims=True) + NORM_EPS)
    scale = jnp.where(j < GDN_HEADS, inv * (GDN_DK ** -0.5), jnp.where(j < 2 * GDN_HEADS, inv, 1.0))
    o_ref[...] = y * scale


def _gdn_prep(p_main, conv_w, il, row0, nb, n, prev=None):
    col0 = FNET_WIDTH // LANES
    args = [p_main, conv_w]
    in_specs = [
        pl.BlockSpec((n, LANES), lambda b, j: (row0 // n + b, col0 + j)),
        pl.BlockSpec((None, GDN_CONV, LANES), lambda b, j: (il, 0, j)),
    ]
    kwargs = {}
    kern = _gdn_prep_kernel
    if prev is not None:
        args.append(prev)
        in_specs.append(pl.BlockSpec(memory_space=pl.ANY))
        kwargs["input_output_aliases"] = {2: 0}
        kern = _gdn_prep_alias_kernel
    return pl.pallas_call(
        kern,
        grid=(nb, GDN_QKV // LANES),
        in_specs=in_specs,
        out_specs=pl.BlockSpec((n, LANES), lambda b, j: (row0 // n + b, j)),
        out_shape=jax.ShapeDtypeStruct((N_TOK, GDN_QKV), F32),
        compiler_params=_cparams(("arbitrary", "arbitrary")),
        name=f"gdn_prep_{n}",
        **kwargs,
    )(*args)


def _gdn_prep_alias_kernel(x_ref, w_ref, prev_ref, o_ref):
    del prev_ref
    _gdn_prep_kernel(x_ref, w_ref, o_ref)


def _split2(a):
    hi = a.astype(BF16)
    lo = (a - hi.astype(F32)).astype(BF16)
    return hi, lo


def _split3(a):
    hi = a.astype(BF16)
    r = a - hi.astype(F32)
    mid = r.astype(BF16)
    lo = (r - mid.astype(F32)).astype(BF16)
    return hi, mid, lo


def _dot_split(a, b):
    ah, al = _split2(a)
    bh, bl = _split2(b)
    return _dot(jnp.concatenate([ah, al, ah], axis=1), jnp.concatenate([bh, bh, bl], axis=0))


def _gdn_kernel(*refs, rev, has_s0, finish, emit_state, ntb):
    it = iter(refs)
    q_ref, k_ref, v_ref, tail_ref, tailt_ref = next(it), next(it), next(it), next(it), next(it)
    arow_ref, dtrow_ref, acol_ref, dtcol_ref = next(it), next(it), next(it), next(it)
    s0_ref = next(it) if has_s0 else None
    if finish:
        of_ref, z_ref, ng_ref = next(it), next(it), next(it)
    o_ref = next(it)
    sfin_ref = next(it) if emit_state else None
    s_scr = next(it)

    hg = pl.program_id(1)
    tb = pl.program_id(2)
    d = 1 if rev else 0
    c = GDN_CHUNK
    ncb = GDN_TB // c
    hp = GDN_HP

    @pl.when(tb == 0)
    def _():
        if has_s0:
            s_scr[...] = s0_ref[...]
        else:
            s_scr[...] = jnp.zeros_like(s_scr)

    tail = tail_ref[...]
    lane = lax.broadcasted_iota(jnp.int32, tail.shape, 1)
    g_all = -jnp.exp(arow_ref[...]) * _softplus(tail + dtrow_ref[...])
    beta_all = _sigmoid(tail)
    tail_t = tailt_ref[...]
    sub = lax.broadcasted_iota(jnp.int32, tail_t.shape, 0)
    g_all_t = -jnp.exp(acol_ref[...]) * _softplus(tail_t + dtcol_ref[...])

    ri = lax.broadcasted_iota(jnp.int32, (c, c), 0)
    ci = lax.broadcasted_iota(jnp.int32, (c, c), 1)
    eye = (ri == ci).astype(F32)
    if rev:
        m_incl = ri <= ci
        m_strict = ri < ci
    else:
        m_incl = ri >= ci
        m_strict = ri > ci
    cum_col = m_incl.astype(BF16)
    cum_row = ((ri >= ci) if rev else (ri <= ci)).astype(BF16)
    cum_col3 = jnp.concatenate([cum_col] * 3, axis=1)
    cum_row3 = jnp.concatenate([cum_row] * 3, axis=0)

    chunk_order = list(range(ncb - 1, -1, -1)) if rev else list(range(ncb))

    units = [(hh, cc) for hh in range(hp) for cc in chunk_order]

    def cols_of(hh):
        return slice(hh * LANES, (hh + 1) * LANES)

    def rows_of(cc):
        return slice(cc * c, (cc + 1) * c)

    heads = {}
    for hh in range(hp):
        gcol = 2 * GDN_HEADS + d * GDN_HEADS + hg * hp + hh
        bcol = d * GDN_HEADS + hg * hp + hh
        heads[hh] = (
            jnp.sum(jnp.where(lane == gcol, g_all, 0.0), axis=1, keepdims=True),
            jnp.sum(jnp.where(lane == bcol, beta_all, 0.0), axis=1, keepdims=True),
            jnp.sum(jnp.where(sub == gcol, g_all_t, 0.0), axis=0, keepdims=True),
        )

    gc = {}
    for hh, cc in units:
        g_colv, _, g_rowv = heads[hh]
        g_c = jnp.broadcast_to(g_colv[rows_of(cc), :], (c, LANES))
        gcum = _dot(cum_col3, jnp.concatenate(_split3(g_c), axis=0))
        g_r = jnp.broadcast_to(g_rowv[:, cc * c:(cc + 1) * c], (8, c))
        gcum_r = _dot(jnp.concatenate(_split3(g_r), axis=1), cum_row3)[0:1, :]
        gc[hh, cc] = (gcum, gcum_r)

    st = {}
    for hh, cc in units:
        gcum, gcum_r = gc[hh, cc]
        q = q_ref[rows_of(cc), cols_of(hh)]
        k = k_ref[rows_of(cc), cols_of(hh)]
        decay = jnp.where(m_incl, jnp.exp(jnp.minimum(gcum[:, :c] - gcum_r, 0.0)), 0.0)
        beta = jnp.broadcast_to(heads[hh][1][rows_of(cc), :], (c, LANES))
        kb = k * beta
        both = _dot_nt(jnp.concatenate([kb, q], axis=0).astype(BF16), k.astype(BF16))
        n = -jnp.where(m_strict, both[:c] * decay, 0.0)
        st[hh, cc] = dict(attn=(both[c:] * decay).astype(BF16), n=n, x=eye + n, kb=kb, beta=beta)

    for u in units:
        st[u]["p"] = _dot_split(st[u]["n"], st[u]["n"])
    levels = int(math.log2(c)) - 1
    for lvl in range(levels):
        for u in units:
            s_u = st[u]
            if lvl == levels - 1:
                s_u["x"] = s_u["x"] + _dot_split(s_u["p"], s_u["x"])
            else:
                xp = _dot_split(s_u["p"], jnp.concatenate([s_u["x"], s_u["p"]], axis=1))
                s_u["x"] = s_u["x"] + xp[:, :c]
                s_u["p"] = xp[:, c:]

    local = {}
    for hh, cc in units:
        s_u = st[hh, cc]
        gcum, _ = gc[hh, cc]
        g_last = gcum[0:1, :] if rev else gcum[c - 1:c, :]
        q = q_ref[rows_of(cc), cols_of(hh)]
        k = k_ref[rows_of(cc), cols_of(hh)]
        v = v_ref[rows_of(cc), cols_of(hh)]
        e_g = jnp.exp(gcum)
        uw = _dot_split(s_u["x"], jnp.concatenate([v * s_u["beta"], s_u["kb"] * e_g], axis=1))
        wq = jnp.concatenate([uw[:, GDN_DV:], q * e_g], axis=0).astype(BF16)
        k_dec = (k * jnp.exp(g_last - gcum)).astype(BF16)
        local[hh, cc] = (uw[:, :GDN_DV], wq, s_u["attn"], k_dec, jnp.exp(g_last))

    states = [s_scr[hh] for hh in range(hp)]
    for cc in chunk_order:
        for hh in range(hp):
            rows, cols = rows_of(cc), cols_of(hh)
            s = states[hh]
            u, wq, attn, k_dec, e_last = local[hh, cc]
            ws = _dot(wq, s.astype(BF16))
            vb = (u - ws[:c]).astype(BF16)
            o = ws[c:] + _dot(attn, vb)
            states[hh] = s * e_last + _dot_tn(k_dec, vb)
            if finish:
                o = o + of_ref[rows, cols]
                z = z_ref[rows, cols]
                o = o * lax.rsqrt(jnp.mean(o * o, axis=-1, keepdims=True) + NORM_EPS) * ng_ref[...]
                o_ref[rows, cols] = (o * (z * _sigmoid(z))).astype(o_ref.dtype)
            else:
                o_ref[rows, cols] = o
    for hh in range(hp):
        s_scr[hh] = states[hh]

    if emit_state:
        @pl.when(tb == ntb - 1)
        def _():
            sfin_ref[...] = s_scr[...]


def _gdn_scan(qkv, tail, tail_t, gate_params, il, row0, nb, n, *, rev, s0=None, o_fwd=None, z_src=None,
              norm_g=None, emit_state=False, prev=None):
    tb_sz = GDN_TB
    ntb = n // tb_sz
    finish = o_fwd is not None
    d = 1 if rev else 0
    a_row, dt_row, a_col, dt_col = gate_params

    def rowblk(b, t):
        tt = (ntb - 1 - t) if rev else t
        return row0 // tb_sz + b * ntb + tt

    hp = GDN_HP
    hw = hp * LANES
    ngrp = GDN_HEADS // hp
    args = [qkv, qkv, qkv, tail, tail_t, a_row, dt_row, a_col, dt_col]
    in_specs = [
        pl.BlockSpec((tb_sz, hw), lambda b, h, t: (rowblk(b, t), h)),
        pl.BlockSpec((tb_sz, hw), lambda b, h, t: (rowblk(b, t), ngrp + h)),
        pl.BlockSpec((tb_sz, hw), lambda b, h, t: (rowblk(b, t), 2 * ngrp + h)),
        pl.BlockSpec((tb_sz, LANES), lambda b, h, t: (rowblk(b, t), 0)),
        pl.BlockSpec((LANES, tb_sz), lambda b, h, t: (0, rowblk(b, t))),
        pl.BlockSpec((None, 1, LANES), lambda b, h, t: (il, 0, 0)),
        pl.BlockSpec((None, 1, LANES), lambda b, h, t: (il, 0, 0)),
        pl.BlockSpec((None, LANES, 1), lambda b, h, t: (il, 0, 0)),
        pl.BlockSpec((None, LANES, 1), lambda b, h, t: (il, 0, 0)),
    ]
    if s0 is not None:
        args.append(s0)
        in_specs.append(pl.BlockSpec((None, None, None, hp, GDN_DK, GDN_DV),
                                     lambda b, h, t: (b, il, d, h, 0, 0)))
    if finish:
        zcol0 = (FNET_WIDTH + GDN_QKV) // hw
        args += [o_fwd, z_src, norm_g]
        in_specs += [
            pl.BlockSpec((tb_sz, hw), lambda b, h, t: (rowblk(b, t), h)),
            pl.BlockSpec((tb_sz, hw), lambda b, h, t: (rowblk(b, t), zcol0 + h)),
            pl.BlockSpec((None, 1, GDN_DV), lambda b, h, t: (il, 0, 0)),
        ]
    out_dtype = BF16 if finish else F32
    out_shape = [jax.ShapeDtypeStruct((N_TOK, GDN_WIDTH), out_dtype)]
    out_specs = [pl.BlockSpec((tb_sz, hw), lambda b, h, t: (rowblk(b, t), h))]
    if emit_state:
        out_shape.append(jax.ShapeDtypeStruct((nb, GDN_HEADS, GDN_DK, GDN_DV), F32))
        out_specs.append(pl.BlockSpec((None, hp, GDN_DK, GDN_DV), lambda b, h, t: (b, h, 0, 0)))
    kwargs = {}
    n_in = len(args)
    kern = functools.partial(_gdn_kernel, rev=rev, has_s0=s0 is not None, finish=finish,
                             emit_state=emit_state, ntb=ntb)
    if prev is not None:
        args.append(prev)
        in_specs.append(pl.BlockSpec(memory_space=pl.ANY))
        kwargs["input_output_aliases"] = {n_in: 0}
        kern = functools.partial(_gdn_alias_kernel, n_in=n_in, inner=kern)
    res = pl.pallas_call(
        kern,
        grid=(nb, ngrp, ntb),
        in_specs=in_specs,
        out_specs=out_specs,
        out_shape=out_shape,
        scratch_shapes=[pltpu.VMEM((hp, GDN_DK, GDN_DV), F32)],
        compiler_params=_cparams(("arbitrary", "arbitrary", "arbitrary")),
        name=f"gdn_{'bwd' if rev else 'fwd'}_{n}",
        **kwargs,
    )(*args)
    return res


def _gdn_alias_kernel(*refs, n_in, inner):
    inner(*(refs[:n_in] + refs[n_in + 1:]))


def _attn_kernel(*refs, lam_init, has_cache):
    if has_cache:
        q_ref, k_ref, v_ref, kc_ref, vc_ref, lam_ref, g_ref, o_ref = refs
    else:
        q_ref, k_ref, v_ref, lam_ref, g_ref, o_ref = refs
    lv = lam_ref[...]
    lam = (jnp.exp(jnp.sum(lv[0:1, :] * lv[1:2, :], axis=-1, keepdims=True))
           - jnp.exp(jnp.sum(lv[2:3, :] * lv[3:4, :], axis=-1, keepdims=True)) + lam_init)
    q = q_ref[...] * (DIFF_DH ** -0.5)
    lane = lax.broadcasted_iota(jnp.int32, q.shape, 1)
    qs = (jnp.where(lane < DIFF_DH, q, 0.0).astype(BF16), jnp.where(lane >= DIFF_DH, q, 0.0).astype(BF16))
    kb = k_ref[...].astype(BF16)
    kcb = kc_ref[...].astype(BF16) if has_cache else None
    probs = []
    for qh in qs:
        s = _dot_nt(qh, kb)
        m = jnp.max(s, axis=-1, keepdims=True)
        if has_cache:
            sc = _dot_nt(qh, kcb)
            m = jnp.maximum(m, jnp.max(sc, axis=-1, keepdims=True))
            ec = jnp.exp(sc - m)
        e = jnp.exp(s - m)
        den = jnp.sum(e, axis=-1, keepdims=True)
        if has_cache:
            den = den + jnp.sum(ec, axis=-1, keepdims=True)
        inv = 1.0 / den
        probs.append((e * inv, ec * inv if has_cache else None))
    a = probs[0][0] - lam * probs[1][0]
    o = _dot(a.astype(BF16), v_ref[...].astype(BF16))
    if has_cache:
        ac = probs[0][1] - lam * probs[1][1]
        o = o + _dot(ac.astype(BF16), vc_ref[...].astype(BF16))
    o = o * lax.rsqrt(jnp.mean(o * o, axis=-1, keepdims=True) + NORM_EPS) * g_ref[...] * (1.0 - lam_init)
    o_ref[...] = o.astype(o_ref.dtype)


def _attn_prompt(p_odd, lam_rows, norm_g, il, lam_init):
    n = SEQ
    return pl.pallas_call(
        functools.partial(_attn_kernel, lam_init=lam_init, has_cache=False),
        grid=(BATCH, DIFF_HEADS),
        in_specs=[
            pl.BlockSpec((n, LANES), lambda b, h: (b, h)),
            pl.BlockSpec((n, LANES), lambda b, h: (b, DIFF_HEADS + h)),
            pl.BlockSpec((n, LANES), lambda b, h: (b, 2 * DIFF_HEADS + h)),
            pl.BlockSpec((None, 8, LANES), lambda b, h: (il, 0, 0)),
            pl.BlockSpec((None, 1, DIFF_DV), lambda b, h: (il, 0, 0)),
        ],
        out_specs=pl.BlockSpec((n, LANES), lambda b, h: (b, h)),
        out_shape=jax.ShapeDtypeStruct((N_TOK, DIFF_WIDTH), BF16),
        compiler_params=_cparams(("arbitrary", "arbitrary")),
        name="attn_prompt",
    )(p_odd, p_odd, p_odd, lam_rows, norm_g)


def _attn_sample_kernel(*refs, lam_init):
    _attn_kernel(*(refs[:7] + refs[8:]), lam_init=lam_init, has_cache=True)


def _attn_sample(p_odd, cache_k, cache_v, lam_rows, norm_g, il, lam_init, o_prev):
    tq = TQ_ATT
    nq = DEC_SEQ // tq
    kblk = N_P // DEC_SEQ
    return pl.pallas_call(
        functools.partial(_attn_sample_kernel, lam_init=lam_init),
        grid=(DEC_BATCH, DIFF_HEADS, nq),
        in_specs=[
            pl.BlockSpec((tq, LANES), lambda b, h, i: (N_P // tq + b * nq + i, h)),
            pl.BlockSpec((DEC_SEQ, LANES), lambda b, h, i: (kblk + b, DIFF_HEADS + h)),
            pl.BlockSpec((DEC_SEQ, LANES), lambda b, h, i: (kblk + b, 2 * DIFF_HEADS + h)),
            pl.BlockSpec((None, None, PAST_LEN, LANES), lambda b, h, i: (b, il, 0, h)),
            pl.BlockSpec((None, None, PAST_LEN, LANES), lambda b, h, i: (b, il, 0, h)),
            pl.BlockSpec((None, 8, LANES), lambda b, h, i: (il, 0, 0)),
            pl.BlockSpec((None, 1, DIFF_DV), lambda b, h, i: (il, 0, 0)),
            pl.BlockSpec(memory_space=pl.ANY),
        ],
        out_specs=pl.BlockSpec((tq, LANES), lambda b, h, i: (N_P // tq + b * nq + i, h)),
        out_shape=jax.ShapeDtypeStruct((N_TOK, DIFF_WIDTH), BF16),
        input_output_aliases={7: 0},
        compiler_params=_cparams(("arbitrary", "arbitrary", "arbitrary")),
        name="attn_sample",
    )(p_odd, p_odd, p_odd, cache_k, cache_v, lam_rows, norm_g, o_prev)


def _outproj_kernel(*refs, n_parts, ln_row):
    parts = refs[:n_parts]
    ws = refs[n_parts:2 * n_parts]
    x_ref, mod_ref, lng_ref, lnb_ref, o_ref = refs[2 * n_parts:]
    y = _dot(parts[0][...], ws[0][...])
    for p_ref, w_ref in zip(parts[1:], ws[1:]):
        y = y + _dot(p_ref[...], w_ref[...])
    r = ALPHA * x_ref[...] + mod_ref[2:3, :] * y
    o_ref[...] = _layer_norm(r, lng_ref[ln_row:ln_row + 1, :], lnb_ref[ln_row:ln_row + 1, :])


def _outproj_ln(parts, w_all, il, x, mod_all, layer, ln_g, ln_b):
    tm = TM_LN
    n_parts = len(parts)
    in_specs = []
    for p in parts:
        in_specs.append(pl.BlockSpec((tm, p.shape[1]), lambda i: (i, 0)))
    off = 0
    for idx, p in enumerate(parts):
        wdt = p.shape[1]
        in_specs.append(pl.BlockSpec((None, wdt, D_MODEL), functools.partial(lambda i, o: (il, o, 0), o=off // wdt)))
        off += wdt
    in_specs += [
        pl.BlockSpec((tm, D_MODEL), lambda i: (i, 0)),
        pl.BlockSpec((None, None, 8, D_MODEL), lambda i: (layer, _group_of_row(i * tm), 0, 0)),
        pl.BlockSpec((2 * DEPTH, D_MODEL), lambda i: (0, 0)),
        pl.BlockSpec((2 * DEPTH, D_MODEL), lambda i: (0, 0)),
    ]
    return pl.pallas_call(
        functools.partial(_outproj_kernel, n_parts=n_parts, ln_row=2 * layer),
        grid=(N_TOK // tm,),
        in_specs=in_specs,
        out_specs=pl.BlockSpec((tm, D_MODEL), lambda i: (i, 0)),
        out_shape=jax.ShapeDtypeStruct((N_TOK, D_MODEL), F32),
        compiler_params=_cparams(("arbitrary",)),
        name="outproj_ln",
    )(*parts, *([w_all] * n_parts), x, mod_all, ln_g, ln_b)


def _ffn_dense_kernel(x_ref, mod_ref, lng_ref, lnb_ref, wg_ref, wu_ref, wd_ref, o_ref, h_scr, acc_scr, *,
                      ln_row, nf):
    j = pl.program_id(1)

    @pl.when(j == 0)
    def _():
        h_scr[...] = _modulated(x_ref, mod_ref, 3).astype(BF16)
        acc_scr[...] = jnp.zeros_like(acc_scr)

    hb = h_scr[...]
    g = _dot(hb, wg_ref[...])
    u = _dot(hb, wu_ref[...])
    a = (g * _sigmoid(g) * u).astype(BF16)
    acc_scr[...] += _dot(a, wd_ref[...])

    @pl.when(j == nf - 1)
    def _():
        r = ALPHA * x_ref[...] + mod_ref[5:6, :] * acc_scr[...]
        o_ref[...] = _layer_norm(r, lng_ref[ln_row:ln_row + 1, :], lnb_ref[ln_row:ln_row + 1, :])


def _ffn_dense(x, mod_all, layer, ln_g, ln_b, w_gu, w_down, il):
    tm, tf = TM_FFN, TF_FFN
    nf = FFN_DIM // tf
    return pl.pallas_call(
        functools.partial(_ffn_dense_kernel, ln_row=2 * layer + 1, nf=nf),
        grid=(N_TOK // tm, nf),
        in_specs=[
            pl.BlockSpec((tm, D_MODEL), lambda i, j: (i, 0)),
            pl.BlockSpec((None, None, 8, D_MODEL), lambda i, j: (layer, _group_of_row(i * tm), 0, 0)),
            pl.BlockSpec((2 * DEPTH, D_MODEL), lambda i, j: (0, 0)),
            pl.BlockSpec((2 * DEPTH, D_MODEL), lambda i, j: (0, 0)),
            pl.BlockSpec((None, D_MODEL, tf), lambda i, j: (il, 0, j)),
            pl.BlockSpec((None, D_MODEL, tf), lambda i, j: (il, 0, nf + j)),
            pl.BlockSpec((None, tf, D_MODEL), lambda i, j: (il, j, 0)),
        ],
        out_specs=pl.BlockSpec((tm, D_MODEL), lambda i, j: (i, 0)),
        out_shape=jax.ShapeDtypeStruct((N_TOK, D_MODEL), F32),
        scratch_shapes=[pltpu.VMEM((tm, D_MODEL), BF16), pltpu.VMEM((tm, D_MODEL), F32)],
        compiler_params=_cparams(("arbitrary", "arbitrary")),
        name="ffn_dense",
    )(x, mod_all, ln_g, ln_b, w_gu, w_gu, w_down)


def _router_kernel(x_ref, mod_ref, wr_ref, h_ref, idx_ref, gate_ref):
    h = _modulated(x_ref, mod_ref, 3)
    h_ref[...] = h
    logits = _dot(h, wr_ref[...], HI)
    lane_i = lax.broadcasted_iota(jnp.int32, logits.shape, 1)
    lane = lane_i.astype(F32)
    neg = -1e30
    logits = jnp.where(lane_i < N_EXPERTS, logits, neg)
    m1 = jnp.max(logits, axis=-1, keepdims=True)
    i1 = jnp.min(jnp.where(logits == m1, lane, float(LANES)), axis=-1, keepdims=True)
    rest = jnp.where(lane == i1, neg, logits)
    m2 = jnp.max(rest, axis=-1, keepdims=True)
    i2 = jnp.min(jnp.where(rest == m2, lane, float(LANES)), axis=-1, keepdims=True)
    e = jnp.exp(m2 - m1)
    g1 = 1.0 / (1.0 + e)
    g2 = e / (1.0 + e)
    idx_ref[...] = jnp.where(lane_i == 0, i1, jnp.where(lane_i == 1, i2, 0.0)).astype(jnp.int32)
    gate_ref[...] = jnp.where(lane_i == 0, g1, jnp.where(lane_i == 1, g2, 0.0))


def _router(x, mod_all, layer, router_w_pad, il):
    tm = TM_TOK
    return pl.pallas_call(
        _router_kernel,
        grid=(N_TOK // tm,),
        in_specs=[
            pl.BlockSpec((tm, D_MODEL), lambda i: (i, 0)),
            pl.BlockSpec((None, None, 8, D_MODEL), lambda i: (layer, _group_of_row(i * tm), 0, 0)),
            pl.BlockSpec((None, D_MODEL, LANES), lambda i: (il, 0, 0)),
        ],
        out_specs=[
            pl.BlockSpec((tm, D_MODEL), lambda i: (i, 0)),
            pl.BlockSpec((tm, LANES), lambda i: (i, 0)),
            pl.BlockSpec((tm, LANES), lambda i: (i, 0)),
        ],
        out_shape=[
            jax.ShapeDtypeStruct((N_TOK, D_MODEL), F32),
            jax.ShapeDtypeStruct((N_TOK, LANES), jnp.int32),
            jax.ShapeDtypeStruct((N_TOK, LANES), F32),
        ],
        compiler_params=_cparams(("arbitrary",)),
        name="router",
    )(x, mod_all, router_w_pad)


def _routing_plan(idx):
    tm = TM_MOE
    e_flat = idx[:, :TOP_K].reshape(-1)
    onehot = (e_flat[:, None] == jnp.arange(N_EXPERTS, dtype=jnp.int32)[None, :]).astype(jnp.int32)
    csum = jnp.cumsum(onehot, axis=0)
    rank = jnp.take_along_axis(csum, e_flat[:, None], axis=1)[:, 0] - 1
    counts = csum[-1]
    tiles_per = (counts + tm - 1) // tm
    tile_end = jnp.cumsum(tiles_per)
    tile_start = tile_end - tiles_per
    pos = tile_start[e_flat] * tm + rank
    n_used = tile_end[-1]
    tile_ids = jnp.arange(MOE_TILES, dtype=jnp.int32)
    tile_expert = jnp.sum((tile_ids[:, None] >= tile_end[None, :]).astype(jnp.int32), axis=1)
    last_expert = jnp.max(jnp.where(counts > 0, jnp.arange(N_EXPERTS, dtype=jnp.int32), 0))
    tile_expert = jnp.minimum(tile_expert, last_expert).astype(jnp.int32)
    tok = jnp.arange(TOP_K * N_TOK, dtype=jnp.int32) // TOP_K
    row_tok = jnp.zeros((MOE_ROWS,), jnp.int32).at[pos].set(tok, unique_indices=True)
    return pos.astype(jnp.int32), row_tok, tile_expert, n_used.astype(jnp.int32).reshape(1)


def _ffn_moe_kernel(texp_ref, nused_ref, rowtok_ref, h_hbm, wg_ref, wu_ref, wd_ref, o_ref,
                    h_scr, acc_scr, sem, *, tm, nf):
    i = pl.program_id(0)
    j = pl.program_id(1)
    used = i < nused_ref[0]

    def row_copy(r):
        return pltpu.make_async_copy(h_hbm.at[pl.ds(rowtok_ref[i * tm + r], 1), :],
                                     acc_scr.at[pl.ds(r, 1), :], sem)

    @pl.when(jnp.logical_and(used, j == 0))
    def _():
        def start(r, carry):
            row_copy(r).start()
            return carry

        lax.fori_loop(0, tm, start, 0)

        def wait(r, carry):
            row_copy(r).wait()
            return carry

        lax.fori_loop(0, tm, wait, 0)
        h_scr[...] = acc_scr[...].astype(BF16)
        acc_scr[...] = jnp.zeros_like(acc_scr)

    @pl.when(used)
    def _():
        hb = h_scr[...]
        g = _dot(hb, wg_ref[...].astype(BF16))
        u = _dot(hb, wu_ref[...].astype(BF16))
        a = (g * _sigmoid(g) * u).astype(BF16)
        acc_scr[...] += _dot(a, wd_ref[...].astype(BF16))

    @pl.when(jnp.logical_and(used, j == nf - 1))
    def _():
        o_ref[...] = acc_scr[...]

    @pl.when(jnp.logical_and(jnp.logical_not(used), j == nf - 1))
    def _():
        o_ref[...] = jnp.zeros_like(o_ref)


def _ffn_moe(h, plan, w_gu, w_down, il):
    _, row_tok, tile_expert, n_used = plan
    tm, tf = TM_MOE, TF_MOE
    nf = EXPERT_DIM // tf

    def wmap(off):
        def f(i, j, texp, nused, rowtok):
            live = i < nused[0]
            ii = jnp.minimum(i, nused[0] - 1)
            return (il, texp[ii], 0, off + jnp.where(live, j, nf - 1))
        return f

    def dmap(i, j, texp, nused, rowtok):
        live = i < nused[0]
        ii = jnp.minimum(i, nused[0] - 1)
        return (il, texp[ii], jnp.where(live, j, nf - 1), 0)

    grid_spec = pltpu.PrefetchScalarGridSpec(
        num_scalar_prefetch=3,
        grid=(MOE_TILES, nf),
        in_specs=[
            pl.BlockSpec(memory_space=pl.ANY),
            pl.BlockSpec((None, None, D_MODEL, tf), wmap(0)),
            pl.BlockSpec((None, None, D_MODEL, tf), wmap(nf)),
            pl.BlockSpec((None, None, tf, D_MODEL), dmap),
        ],
        out_specs=pl.BlockSpec((tm, D_MODEL), lambda i, j, *_: (i, 0), pipeline_mode=pl.Buffered(1)),
        scratch_shapes=[
            pltpu.VMEM((tm, D_MODEL), BF16),
            pltpu.VMEM((tm, D_MODEL), F32),
            pltpu.SemaphoreType.DMA(()),
        ],
    )
    return pl.pallas_call(
        functools.partial(_ffn_moe_kernel, tm=tm, nf=nf),
        grid_spec=grid_spec,
        out_shape=jax.ShapeDtypeStruct((MOE_ROWS, D_MODEL), F32),
        compiler_params=_cparams(("arbitrary", "arbitrary")),
        name="ffn_moe",
    )(tile_expert, n_used, row_tok, h, w_gu, w_gu, w_down)


def _combine_kernel(pos_ref, y_hbm, gate_ref, x_ref, mod_ref, lng_ref, lnb_ref, o_ref, ybuf, sem, *, tm, ln_row):
    i = pl.program_id(0)

    def row_copy(r, k):
        return pltpu.make_async_copy(y_hbm.at[pl.ds(pos_ref[TOP_K * (i * tm + r) + k], 1), :],
                                     ybuf.at[k, pl.ds(r, 1), :], sem)

    def start(r, carry):
        for k in range(TOP_K):
            row_copy(r, k).start()
        return carry

    lax.fori_loop(0, tm, start, 0)

    def wait(r, carry):
        for k in range(TOP_K):
            row_copy(r, k).wait()
        return carry

    lax.fori_loop(0, tm, wait, 0)
    gates = gate_ref[...]
    f = gates[:, 0:1] * ybuf[0] + gates[:, 1:2] * ybuf[1]
    r = ALPHA * x_ref[...] + mod_ref[5:6, :] * f
    o_ref[...] = _layer_norm(r, lng_ref[ln_row:ln_row + 1, :], lnb_ref[ln_row:ln_row + 1, :])


def _moe_combine(y_rows, pos, gates, x, mod_all, layer, ln_g, ln_b):
    tm = TM_TOK
    grid_spec = pltpu.PrefetchScalarGridSpec(
        num_scalar_prefetch=1,
        grid=(N_TOK // tm,),
        in_specs=[
            pl.BlockSpec(memory_space=pl.ANY),
            pl.BlockSpec((tm, LANES), lambda i, *_: (i, 0)),
            pl.BlockSpec((tm, D_MODEL), lambda i, *_: (i, 0)),
            pl.BlockSpec((None, None, 8, D_MODEL), lambda i, *_: (layer, _group_of_row(i * tm), 0, 0)),
            pl.BlockSpec((2 * DEPTH, D_MODEL), lambda i, *_: (0, 0)),
            pl.BlockSpec((2 * DEPTH, D_MODEL), lambda i, *_: (0, 0)),
        ],
        out_specs=pl.BlockSpec((tm, D_MODEL), lambda i, *_: (i, 0)),
        scratch_shapes=[pltpu.VMEM((TOP_K, tm, D_MODEL), F32), pltpu.SemaphoreType.DMA(())],
    )
    return pl.pallas_call(
        functools.partial(_combine_kernel, tm=tm, ln_row=2 * layer + 1),
        grid_spec=grid_spec,
        out_shape=jax.ShapeDtypeStruct((N_TOK, D_MODEL), F32),
        compiler_params=_cparams(("arbitrary",)),
        name="moe_combine",
    )(pos, y_rows, gates, x, mod_all, ln_g, ln_b)


def _pad_lanes(a, lane0=0):
    width = a.shape[-1]
    pads = [(0, 0)] * (a.ndim - 1) + [(lane0, LANES - lane0 - width)]
    return jnp.pad(a, pads)


def kernel(x_prompt, x_sample, state_gdn, cache_diff_k, cache_diff_v, c, c_ctx, w_ada, b_ada, ln_g, ln_b,
           w_in_even, gdn_conv_w, gdn_a_log, gdn_dt_bias, gdn_norm_g, w_out_even, w_in_odd,
           lam_q1, lam_k1, lam_q2, lam_k2, diff_norm_g, w_out_odd, ffn_w_gu, ffn_w_down,
           router_w, moe_w_gu, moe_w_down):
    x = jnp.concatenate([x_prompt.reshape(N_P, D_MODEL), x_sample.reshape(N_S, D_MODEL)], axis=0)

    cond8 = jnp.concatenate([c_ctx[None, :], c, jnp.zeros((8 - N_GROUPS, D_MODEL), F32)], axis=0)
    mod = _ada_modulation(cond8, w_ada, b_ada)
    mod_all = jnp.pad(mod.reshape(DEPTH, 8, 6, D_MODEL)[:, :N_GROUPS], ((0, 0), (0, 0), (0, 2), (0, 0)))

    ln_g2 = ln_g.reshape(2 * DEPTH, D_MODEL)
    ln_b2 = ln_b.reshape(2 * DEPTH, D_MODEL)

    w_even_main = w_in_even[:, :, :EVEN_MAIN].astype(BF16)
    w_even_tail = _pad_lanes(w_in_even[:, :, EVEN_MAIN:]).astype(BF16)
    w_even_tail_t = jnp.swapaxes(w_even_tail, 1, 2)
    w_out_even_b = w_out_even.astype(BF16)
    w_in_odd_b = w_in_odd.astype(BF16)
    w_out_odd_b = w_out_odd.astype(BF16)
    ffn_w_gu_b = ffn_w_gu.astype(BF16)
    ffn_w_down_b = ffn_w_down.astype(BF16)
    router_w_pad = _pad_lanes(router_w)

    a_row = _pad_lanes(gdn_a_log.reshape(N_EVEN, 1, 2 * GDN_HEADS), 2 * GDN_HEADS)
    dt_row = _pad_lanes(gdn_dt_bias.reshape(N_EVEN, 1, 2 * GDN_HEADS), 2 * GDN_HEADS)
    gate_params = (a_row, dt_row, jnp.swapaxes(a_row, 1, 2), jnp.swapaxes(dt_row, 1, 2))
    gdn_norm3 = gdn_norm_g.reshape(N_EVEN, 1, GDN_DV)

    rope = _rope_tables()
    lam_rows = jnp.pad(jnp.stack([lam_q1, lam_k1, lam_q2, lam_k2], axis=1),
                       ((0, 0), (0, 4), (0, LANES - DIFF_DH)))
    diff_norm3 = diff_norm_g.reshape(N_ODD, 1, DIFF_DV)
    cache_k = cache_diff_k.reshape(DEC_BATCH, N_ODD, PAST_LEN, DIFF_WIDTH)
    cache_v = cache_diff_v.reshape(DEC_BATCH, N_ODD, PAST_LEN, DIFF_WIDTH)

    gdn_states, ctx_keys, ctx_vals = [], [], []
    for layer in range(DEPTH):
        il = layer // 2
        if layer % 2 == 0:
            p_main, tail, tail_t = _inproj_even(x, mod_all, layer, w_even_main, w_even_tail, w_even_tail_t, il)
            y_a = _fnet(p_main, 0, BATCH, SEQ)
            y_a = _fnet(p_main, N_P, DEC_BATCH, DEC_SEQ, y_prev=y_a)
            qkv = _gdn_prep(p_main, gdn_conv_w, il, 0, BATCH, SEQ)
            qkv = _gdn_prep(p_main, gdn_conv_w, il, N_P, DEC_BATCH, DEC_SEQ, prev=qkv)
            common = (qkv, tail, tail_t, gate_params, il)
            o_f, s_f = _gdn_scan(*common, 0, BATCH, SEQ, rev=False, emit_state=True)
            (o_f,) = _gdn_scan(*common, N_P, DEC_BATCH, DEC_SEQ, rev=False, s0=state_gdn, prev=o_f)
            y_b, s_b = _gdn_scan(*common, 0, BATCH, SEQ, rev=True, o_fwd=o_f, z_src=p_main,
                                 norm_g=gdn_norm3, emit_state=True)
            (y_b,) = _gdn_scan(*common, N_P, DEC_BATCH, DEC_SEQ, rev=True, s0=state_gdn, o_fwd=o_f,
                               z_src=p_main, norm_g=gdn_norm3, prev=y_b)
            gdn_states.append(jnp.stack([s_f, s_b], axis=1))
            x = _outproj_ln([y_a, y_b], w_out_even_b, il, x, mod_all, layer, ln_g2, ln_b2)
            x = _ffn_dense(x, mod_all, layer, ln_g2, ln_b2, ffn_w_gu_b, ffn_w_down_b, il)
        else:
            lam_init = 0.8 - 0.6 * math.exp(-0.3 * layer)
            p_odd = _inproj_odd(x, mod_all, layer, w_in_odd_b, il, rope)
            ctx_keys.append(p_odd[:N_P, DIFF_WIDTH:2 * DIFF_WIDTH].reshape(BATCH, SEQ, 2 * DIFF_HEADS, DIFF_DH))
            ctx_vals.append(p_odd[:N_P, 2 * DIFF_WIDTH:].reshape(BATCH, SEQ, DIFF_HEADS, DIFF_DV))
            o = _attn_prompt(p_odd, lam_rows, diff_norm3, il, lam_init)
            o = _attn_sample(p_odd, cache_k, cache_v, lam_rows, diff_norm3, il, lam_init, o)
            x = _outproj_ln([o], w_out_odd_b, il, x, mod_all, layer, ln_g2, ln_b2)
            h, idx, gates = _router(x, mod_all, layer, router_w_pad, il)
            plan = _routing_plan(idx)
            y_rows = _ffn_moe(h, plan, moe_w_gu, moe_w_down, il)
            x = _moe_combine(y_rows, plan[0], gates, x, mod_all, layer, ln_g2, ln_b2)

    y_prompt = x[:N_P].reshape(BATCH, SEQ, D_MODEL)
    y_sample = x[N_P:].reshape(DEC_BATCH, DEC_SEQ, D_MODEL)
    new_state_gdn = jnp.stack(gdn_states, axis=1)
    new_cache_k = jnp.stack(ctx_keys, axis=1)
    new_cache_v = jnp.stack(ctx_vals, axis=1)
    return (y_prompt, y_sample, new_state_gdn, new_cache_k, new_cache_v)
```
